```python
import jax, jax.numpy as jnp
from jax import lax
import numpy as np

D_MODEL = 2048
BATCH = 16
SEQ = 2048
DEPTH = 4

MEM_LEN = 256
D_A = D_MODEL // 2
D_B = D_MODEL - D_A
SGU_CHUNK = 128
SGU_GROUPS = 8
SGU_GROUP_DIM = D_A // SGU_GROUPS
HGRN_HEAD_DIM = 128
HGRN_HEADS = D_B // HGRN_HEAD_DIM
HGRN_CHUNK = 64
EVEN_IN = 2 * D_A + 4 * D_B
CONV_WIDTH = 3
XA_HEADS = 4
XA_HEAD_DIM = D_MODEL // XA_HEADS
PEER_HEADS = 8
PEER_NKEYS = 128
PEER_EXPERTS = PEER_NKEYS * PEER_NKEYS
PEER_QDIM = 256
PEER_HALF = PEER_QDIM // 2
PEER_TOPK = 16
PEER_TOKEN_BLOCK = 128
N_EVEN = (DEPTH + 1) // 2
N_ODD = DEPTH // 2
DN_ALPHA = (2.0 * DEPTH) ** 0.25
DN_BETA = (8.0 * DEPTH) ** -0.25
LN_EPS = 1e-5
F_FLOOR = 1e-30

kernel_name = "hybrid_sgu_hgrn2_shortconv_peer_deepnorm"


def layer_norm(x, g, b):
    xf = x.astype(jnp.float32)
    mu = jnp.mean(xf, -1, keepdims=True)
    var = jnp.mean(jnp.square(xf - mu), -1, keepdims=True)
    return ((xf - mu) * lax.rsqrt(var + LN_EPS) * g + b).astype(x.dtype)


def rms_norm(x, g):
    xf = x.astype(jnp.float32)
    return (xf * lax.rsqrt(jnp.mean(xf * xf, -1, keepdims=True) + LN_EPS) * g).astype(x.dtype)


def spatial_gating(u, v, ln_g, ln_b, w_s, b_s):
    bsz, seq, _ = v.shape
    n_chunks = seq // SGU_CHUNK
    v = layer_norm(v, ln_g, ln_b)
    v = v.reshape(bsz, n_chunks, SGU_CHUNK, SGU_GROUPS, SGU_GROUP_DIM)
    causal = jnp.tril(jnp.ones((SGU_CHUNK, SGU_CHUNK), dtype=bool))
    w = jnp.where(causal, w_s, 0)
    mixed = jnp.einsum('gts,bnsgc->bntgc', w, v) + b_s.T[:, :, None]
    return u * mixed.reshape(bsz, seq, D_A)


def hgrn2(q, f_pre, i, g, lb, onorm_g):
    bsz, seq, _ = q.shape
    nc = seq // HGRN_CHUNK

    def heads(t):
        return t.reshape(bsz, nc, HGRN_CHUNK, HGRN_HEADS, HGRN_HEAD_DIM).transpose(1, 0, 3, 2, 4)

    z = f_pre.astype(jnp.float32)
    lbf = lb.astype(jnp.float32)
    f = lbf + (1.0 - lbf) * jax.nn.sigmoid(z)
    log_f = jnp.log(jnp.maximum(f, F_FLOOR))
    k = (1.0 - lbf) * jax.nn.sigmoid(-z)
    qf = jax.nn.silu(q.astype(jnp.float32))
    vf = i.astype(jnp.float32)
    causal = jnp.tril(jnp.ones((HGRN_CHUNK, HGRN_CHUNK), dtype=bool))[:, :, None]

    def step(state, xs):
        qc, kc, vc, lfc = xs
        b = jnp.cumsum(lfc, axis=2)
        o_inter = jnp.einsum('bhtd,bhde->bhte', qc * jnp.exp(b), state)
        diff = b[:, :, :, None, :] - b[:, :, None, :, :]
        decay = jnp.where(causal, jnp.exp(jnp.where(causal, diff, 0.0)), 0.0)
        scores = jnp.einsum('bhtd,bhsd,bhtsd->bhts', qc, kc, decay)
        o = o_inter + jnp.einsum('bhts,bhse->bhte', scores, vc)
        b_last = b[:, :, -1:, :]
        k_dec = kc * jnp.exp(b_last - b)
        state = jnp.exp(b_last[:, :, 0, :])[..., None] * state + jnp.einsum('bhsd,bhse->bhde', k_dec, vc)
        return state, o

    s0 = jnp.zeros((bsz, HGRN_HEADS, HGRN_HEAD_DIM, HGRN_HEAD_DIM), jnp.float32)
    _, o = lax.scan(step, s0, (heads(qf), heads(k), heads(vf), heads(log_f)))
    o = o.transpose(1, 0, 3, 2, 4).reshape(bsz, seq, HGRN_HEADS, HGRN_HEAD_DIM)
    o = rms_norm(o, onorm_g.reshape(HGRN_HEADS, HGRN_HEAD_DIM))
    return (o.reshape(bsz, seq, D_B) * jax.nn.silu(g.astype(jnp.float32))).astype(q.dtype)


def even_mixer(x, w_in, sgu_g, sgu_b, w_s, b_s, lb, onorm_g, w_out):
    h = x @ w_in
    cuts = [D_A, 2 * D_A, 2 * D_A + D_B, 2 * D_A + 2 * D_B, 2 * D_A + 3 * D_B]
    a_u, a_v, b_q, b_f, b_i, b_g = jnp.split(h, cuts, axis=-1)
    a_out = spatial_gating(jax.nn.gelu(a_u), jax.nn.gelu(a_v), sgu_g, sgu_b, w_s, b_s)
    b_out = hgrn2(b_q, b_f, b_i, b_g, lb, onorm_g)
    return jnp.concatenate([a_out, b_out], axis=-1) @ w_out


def odd_mixer(x, w_in, conv_w, w_out):
    gate_b, gate_c, xin = jnp.split(x @ w_in, 3, axis=-1)
    z = gate_c * xin
    y = lax.conv_general_dilated(z, conv_w[:, None, :], window_strides=(1,),
                                 padding=[(CONV_WIDTH - 1, 0)],
                                 dimension_numbers=('NWC', 'WIO', 'NWC'),
                                 feature_group_count=D_MODEL)
    return (gate_b * y) @ w_out


def memory_attention(x, mem, w_q, w_kv, w_o):
    bsz, seq, _ = x.shape
    m_len = mem.shape[1]
    q = (x @ w_q).reshape(bsz, seq, XA_HEADS, XA_HEAD_DIM)
    k, v = jnp.split(mem @ w_kv, 2, axis=-1)
    k = k.reshape(bsz, m_len, XA_HEADS, XA_HEAD_DIM)
    v = v.reshape(bsz, m_len, XA_HEADS, XA_HEAD_DIM)
    s = jnp.einsum('bshd,bmhd->bhsm', q, k).astype(jnp.float32) * (XA_HEAD_DIM ** -0.5)
    p = jax.nn.softmax(s, axis=-1).astype(v.dtype)
    o = jnp.einsum('bhsm,bmhd->bshd', p, v).reshape(bsz, seq, D_MODEL)
    return o @ w_o


def peer_ffn(x, w_q, sub_keys, u_tab, v_tab):
    bsz, seq, d = x.shape
    t = x.reshape(-1, d)
    q = (t @ w_q).reshape(-1, PEER_HEADS, 2, PEER_HALF)
    s = jnp.einsum('thpc,pkc->thpk', q, sub_keys).astype(jnp.float32)
    top_s, top_i = lax.top_k(s, PEER_TOPK)
    cand_s = (top_s[:, :, 0, :, None] + top_s[:, :, 1, None, :]).reshape(-1, PEER_HEADS, PEER_TOPK * PEER_TOPK)
    cand_i = (top_i[:, :, 0, :, None] * PEER_NKEYS + top_i[:, :, 1, None, :]).reshape(-1, PEER_HEADS, PEER_TOPK * PEER_TOPK)
    best_s, pos = lax.top_k(cand_s, PEER_TOPK)
    expert = jnp.take_along_axis(cand_i, pos, axis=-1)
    gate = jax.nn.softmax(best_s, axis=-1).astype(x.dtype)
    nb = t.shape[0] // PEER_TOKEN_BLOCK

    def block(args):
        tb, eb, gb = args
        act = jax.nn.gelu(jnp.einsum('td,thkd->thk', tb, u_tab[eb])) * gb
        return jnp.einsum('thk,thkd->td', act, v_tab[eb])

    out = lax.map(block, (t.reshape(nb, PEER_TOKEN_BLOCK, d),
                          expert.reshape(nb, PEER_TOKEN_BLOCK, PEER_HEADS, PEER_TOPK),
                          gate.reshape(nb, PEER_TOKEN_BLOCK, PEER_HEADS, PEER_TOPK)))
    return out.reshape(bsz, seq, d)


def setup_inputs(seed: int = 0) -> dict:
    key = jax.random.key(seed)
    ks = iter(jax.random.split(key, 32))

    def nrm(shape, scale):
        return jax.random.normal(next(ks), shape, jnp.float32) * scale

    def gain(shape):
        return 1.0 + nrm(shape, 0.02)

    d = D_MODEL
    xa_k = nrm((DEPTH, d, d), d ** -0.5)
    xa_v = nrm((DEPTH, d, d), d ** -0.5 * DN_BETA)
    return {
        "x": nrm((BATCH, SEQ, d), 1.0),
        "mem": nrm((BATCH, MEM_LEN, d), 1.0),
        "ev_w_in": nrm((N_EVEN, d, EVEN_IN), d ** -0.5),
        "ev_sgu_ln_g": gain((N_EVEN, D_A)),
        "ev_sgu_ln_b": nrm((N_EVEN, D_A), 0.02),
        "ev_w_s": nrm((N_EVEN, SGU_GROUPS, SGU_CHUNK, SGU_CHUNK), SGU_CHUNK ** -0.5),
        "ev_b_s": gain((N_EVEN, SGU_GROUPS, SGU_CHUNK)),
        "ev_lb_logits": nrm((N_EVEN, D_B), 0.5),
        "ev_onorm_g": gain((N_EVEN, D_B)),
        "ev_w_out": nrm((N_EVEN, d, d), d ** -0.5 * DN_BETA),
        "od_w_in": nrm((N_ODD, d, 3 * d), d ** -0.5),
        "od_conv_w": nrm((N_ODD, CONV_WIDTH, d), CONV_WIDTH ** -0.5),
        "od_w_out": nrm((N_ODD, d, d), d ** -0.5 * DN_BETA),
        "mix_ln_g": gain((DEPTH, d)),
        "mix_ln_b": nrm((DEPTH, d), 0.02),
        "xa_w_q": nrm((DEPTH, d, d), d ** -0.5),
        "xa_w_kv": jnp.concatenate([xa_k, xa_v], axis=-1),
        "xa_w_o": nrm((DEPTH, d, d), d ** -0.5 * DN_BETA),
        "xa_ln_g": gain((DEPTH, d)),
        "xa_ln_b": nrm((DEPTH, d), 0.02),
        "peer_w_q": nrm((DEPTH, d, PEER_HEADS * PEER_QDIM), d ** -0.5),
        "peer_keys": nrm((DEPTH, 2, PEER_NKEYS, PEER_HALF), PEER_HALF ** -0.5),
        "peer_u": nrm((DEPTH, PEER_EXPERTS, d), d ** -0.5),
        "peer_v": nrm((DEPTH, PEER_EXPERTS, d), DN_BETA * PEER_HEADS ** -0.5),
        "ffn_ln_g": gain((DEPTH, d)),
        "ffn_ln_b": nrm((DEPTH, d), 0.02),
    }


def reference(x, mem, ev_w_in, ev_sgu_ln_g, ev_sgu_ln_b, ev_w_s, ev_b_s, ev_lb_logits,
              ev_onorm_g, ev_w_out, od_w_in, od_conv_w, od_w_out, mix_ln_g, mix_ln_b,
              xa_w_q, xa_w_kv, xa_w_o, xa_ln_g, xa_ln_b, peer_w_q, peer_keys, peer_u,
              peer_v, ffn_ln_g, ffn_ln_b):
    p = jax.nn.softmax(ev_lb_logits.astype(jnp.float32), axis=0)
    lower_bounds = jnp.clip(jnp.cumsum(p, axis=0) - p[0], 0.0, 1.0)
    h = x
    for layer in range(DEPTH):
        j = layer // 2
        if layer % 2 == 0:
            mix = even_mixer(h, ev_w_in[j], ev_sgu_ln_g[j], ev_sgu_ln_b[j], ev_w_s[j], ev_b_s[j],
                             lower_bounds[j], ev_onorm_g[j], ev_w_out[j])
        else:
            mix = odd_mixer(h, od_w_in[j], od_conv_w[j], od_w_out[j])
        h = layer_norm(DN_ALPHA * h + mix, mix_ln_g[layer], mix_ln_b[layer])
        h = layer_norm(DN_ALPHA * h + memory_attention(h, mem, xa_w_q[layer], xa_w_kv[layer], xa_w_o[layer]),
                       xa_ln_g[layer], xa_ln_b[layer])
        h = layer_norm(DN_ALPHA * h + peer_ffn(h, peer_w_q[layer], peer_keys[layer], peer_u[layer], peer_v[layer]),
                       ffn_ln_g[layer], ffn_ln_b[layer])
    return h
```

```python
import functools

import jax
import jax.numpy as jnp
from jax import lax
from jax.experimental import pallas as pl
from jax.experimental.pallas import tpu as pltpu

LN_EPS = 1e-5
F_FLOOR = 1e-30
HGRN_HEAD_DIM = 128
XA_HEADS = 4
PEER_HEADS = 8
PEER_TOPK = 16
V7X_VMEM_LIMIT_BYTES = 56 * 1024 * 1024
LANES = 128
F32 = jnp.float32
BF16 = jnp.bfloat16


def _params(n_axes):
    return pltpu.CompilerParams(dimension_semantics=("arbitrary",) * n_axes,
                                vmem_limit_bytes=V7X_VMEM_LIMIT_BYTES)


def _tile(n, want):
    t = min(n, want)
    while n % t:
        t -= LANES
    assert t > 0, (n, want)
    return t


def _layer_norm_rows(y, g, b):
    mu = jnp.mean(y, axis=-1, keepdims=True)
    yc = y - mu
    var = jnp.mean(yc * yc, axis=-1, keepdims=True)
    return yc * lax.rsqrt(var + LN_EPS) * g + b


def _mm_kernel(x_ref, w_ref, o_ref):
    o_ref[...] = jnp.dot(x_ref[...], w_ref[...], preferred_element_type=F32).astype(o_ref.dtype)


def _matmul(x, w, out_dtype, tm=1024, tn=1024):
    m, k = x.shape
    n = w.shape[1]
    tm, tn = _tile(m, tm), _tile(n, tn)
    return pl.pallas_call(
        _mm_kernel,
        grid=(n // tn, m // tm),
        in_specs=[pl.BlockSpec((tm, k), lambda j, i: (i, 0)),
                  pl.BlockSpec((k, tn), lambda j, i: (0, j))],
        out_specs=pl.BlockSpec((tm, tn), lambda j, i: (i, j)),
        out_shape=jax.ShapeDtypeStruct((m, n), out_dtype),
        compiler_params=_params(2),
        name="matmul",
    )(x, w)


def _proj_ln_kernel(*refs, n_in, alpha):
    us, ws = refs[:n_in], refs[n_in:2 * n_in]
    h_ref, g_ref, b_ref, of_ref, ob_ref = refs[2 * n_in:]
    acc = jnp.dot(us[0][...], ws[0][...], preferred_element_type=F32)
    for u_ref, w_ref in zip(us[1:], ws[1:]):
        acc = acc + jnp.dot(u_ref[...], w_ref[...], preferred_element_type=F32)
    out = _layer_norm_rows(alpha * h_ref[...] + acc, g_ref[...], b_ref[...])
    of_ref[...] = out
    ob_ref[...] = out.astype(BF16)


def _proj_res_ln(us, ws, h, g, b, alpha, tm=256):
    t, d = h.shape
    tm = _tile(t, tm)
    n_in = len(us)
    in_specs = [pl.BlockSpec((tm, u.shape[1]), lambda i: (i, 0)) for u in us]
    in_specs += [pl.BlockSpec(w.shape, lambda i: (0, 0)) for w in ws]
    in_specs += [pl.BlockSpec((tm, d), lambda i: (i, 0)),
                 pl.BlockSpec((1, d), lambda i: (0, 0)),
                 pl.BlockSpec((1, d), lambda i: (0, 0))]
    return pl.pallas_call(
        functools.partial(_proj_ln_kernel, n_in=n_in, alpha=alpha),
        grid=(t // tm,),
        in_specs=in_specs,
        out_specs=[pl.BlockSpec((tm, d), lambda i: (i, 0)), pl.BlockSpec((tm, d), lambda i: (i, 0))],
        out_shape=[jax.ShapeDtypeStruct((t, d), F32), jax.ShapeDtypeStruct((t, d), BF16)],
        compiler_params=_params(1),
        name="proj_res_ln",
    )(*us, *ws, h, g.reshape(1, d), b.reshape(1, d))


def _res_ln_kernel(h_ref, y_ref, g_ref, b_ref, of_ref, ob_ref, *, alpha):
    out = _layer_norm_rows(alpha * h_ref[...] + y_ref[...], g_ref[...], b_ref[...])
    of_ref[...] = out
    ob_ref[...] = out.astype(BF16)


def _res_ln(h, y, g, b, alpha, tm=256):
    t, d = h.shape
    tm = _tile(t, tm)
    row = pl.BlockSpec((tm, d), lambda i: (i, 0))
    vec = pl.BlockSpec((1, d), lambda i: (0, 0))
    return pl.pallas_call(
        functools.partial(_res_ln_kernel, alpha=alpha),
        grid=(t // tm,),
        in_specs=[row, row, vec, vec],
        out_specs=[row, row],
        out_shape=[jax.ShapeDtypeStruct((t, d), F32), jax.ShapeDtypeStruct((t, d), BF16)],
        compiler_params=_params(1),
        name="res_ln",
    )(h, y, g.reshape(1, d), b.reshape(1, d))


def _shortconv_kernel(x_ref, wb_ref, wc_ref, wx_ref, cw_ref, o_ref, carry_ref, *, tiles_per_seq):
    i = pl.program_id(1)

    @pl.when(i % tiles_per_seq == 0)
    def _():
        carry_ref[...] = jnp.zeros_like(carry_ref)

    x = x_ref[...]
    gate_b = jnp.dot(x, wb_ref[...], preferred_element_type=F32)
    gate_c = jnp.dot(x, wc_ref[...], preferred_element_type=F32)
    xin = jnp.dot(x, wx_ref[...], preferred_element_type=F32)
    z = gate_c * xin
    tm = z.shape[0]
    prev = carry_ref[...]
    p1, p2 = prev[7:8, :], prev[6:7, :]
    row = lax.broadcasted_iota(jnp.int32, z.shape, 0)
    z1 = jnp.where(row == 0, p1, pltpu.roll(z, 1, axis=0))
    z2 = jnp.where(row == 0, p2, jnp.where(row == 1, p1, pltpu.roll(z, 2, axis=0)))
    cw = cw_ref[...]
    y = cw[0:1, :] * z2 + cw[1:2, :] * z1 + cw[2:3, :] * z
    o_ref[...] = (gate_b * y).astype(o_ref.dtype)
    carry_ref[...] = z[tm - 8:tm, :]


def _shortconv(xb, w_in, conv_w, seq, tm=512, tc=512):
    t, d = xb.shape
    tm, tc = _tile(seq, tm), _tile(d, tc)
    nc = d // tc
    cw = jnp.zeros((8, d), F32).at[:conv_w.shape[0]].set(conv_w)
    wspec = lambda off: pl.BlockSpec((d, tc), lambda c, i: (0, off + c))
    return pl.pallas_call(
        functools.partial(_shortconv_kernel, tiles_per_seq=seq // tm),
        grid=(nc, t // tm),
        in_specs=[pl.BlockSpec((tm, d), lambda c, i: (i, 0)),
                  wspec(0), wspec(nc), wspec(2 * nc),
                  pl.BlockSpec((8, tc), lambda c, i: (0, c))],
        out_specs=pl.BlockSpec((tm, tc), lambda c, i: (i, c)),
        out_shape=jax.ShapeDtypeStruct((t, d), BF16),
        scratch_shapes=[pltpu.VMEM((8, tc), F32)],
        compiler_params=_params(2),
        name="shortconv",
    )(xb, w_in, w_in, w_in, cw)


def _sgu_kernel(au_ref, av_ref, g_ref, b_ref, ws_ref, bias_ref, o_ref, *, groups, chunk):
    u = jax.nn.gelu(au_ref[...])
    v = _layer_norm_rows(jax.nn.gelu(av_ref[...]), g_ref[...], b_ref[...]).astype(BF16)
    tm, da = u.shape
    gd = da // groups
    r = lax.broadcasted_iota(jnp.int32, (chunk, chunk), 0)
    c = lax.broadcasted_iota(jnp.int32, (chunk, chunk), 1)
    causal = r >= c
    for g in range(groups):
        w = jnp.where(causal, ws_ref[g], 0.0).astype(BF16)
        cols = slice(g * gd, (g + 1) * gd)
        for n in range(tm // chunk):
            rows = slice(n * chunk, (n + 1) * chunk)
            mixed = jnp.dot(w, v[rows, cols], preferred_element_type=F32) + bias_ref[:, cols]
            o_ref[rows, cols] = (u[rows, cols] * mixed).astype(o_ref.dtype)


def _sgu(hcat, ln_g, ln_b, w_s, b_s, tm=256):
    t = hcat.shape[0]
    groups, chunk, _ = w_s.shape
    da = ln_g.shape[0]
    gd = da // groups
    tm = _tile(t, tm)
    assert tm % chunk == 0
    bias = jnp.repeat(b_s.T, gd, axis=1)
    return pl.pallas_call(
        functools.partial(_sgu_kernel, groups=groups, chunk=chunk),
        grid=(t // tm,),
        in_specs=[pl.BlockSpec((tm, da), lambda i: (i, 0)),
                  pl.BlockSpec((tm, da), lambda i: (i, 1)),
                  pl.BlockSpec((1, da), lambda i: (0, 0)),
                  pl.BlockSpec((1, da), lambda i: (0, 0)),
                  pl.BlockSpec((groups, chunk, chunk), lambda i: (0, 0, 0)),
                  pl.BlockSpec((chunk, da), lambda i: (0, 0))],
        out_specs=pl.BlockSpec((tm, da), lambda i: (i, 0)),
        out_shape=jax.ShapeDtypeStruct((t, da), BF16),
        compiler_params=_params(1),
        name="sgu",
    )(hcat, hcat, ln_g.reshape(1, da), ln_b.reshape(1, da), w_s, bias)


def _hgrn_kernel(q_ref, f_ref, i_ref, g_ref, lb_ref, on_ref, o_ref, state_ref, *, heads, chunk):
    dh = HGRN_HEAD_DIM

    @pl.when(pl.program_id(2) == 0)
    def _():
        state_ref[...] = jnp.zeros_like(state_ref)

    row = lax.broadcasted_iota(jnp.int32, (chunk, dh), 0)
    tt = lax.broadcasted_iota(jnp.int32, (chunk, chunk), 0)
    ss = lax.broadcasted_iota(jnp.int32, (chunk, chunk), 1)
    txs = tt ^ ss
    nt = (((1,), (1,)), ((), ()))
    tn = (((0,), (0,)), ((), ()))

    for h in range(heads):
        cols = slice(h * dh, (h + 1) * dh)
        z = f_ref[:, cols]
        lb = lb_ref[:, cols]
        sig = jax.nn.sigmoid(z)
        f = lb + (1.0 - lb) * sig
        logf = jnp.log(jnp.maximum(f, F_FLOOR))
        k = (1.0 - lb) * jax.nn.sigmoid(-z)
        q = q_ref[:, cols]
        q = q * jax.nn.sigmoid(q)
        v = i_ref[:, cols].astype(BF16)

        b = logf
        sh = 1
        while sh < chunk:
            b = b + jnp.where(row >= sh, pltpu.roll(b, sh, axis=0), 0.0)
            sh *= 2

        scores = jnp.where(tt == ss, lax.dot_general(q.astype(BF16), k.astype(BF16), nt,
                                                     preferred_element_type=F32), 0.0)
        before = jnp.where(row >= 1, pltpu.roll(b, 1, axis=0), 0.0)
        last = b
        m = 1
        while m < chunk:
            qd = (q * jnp.exp(b - before)).astype(BF16)
            kd = (k * jnp.exp(last - b)).astype(BF16)
            lvl = lax.dot_general(qd, kd, nt, preferred_element_type=F32)
            scores = scores + jnp.where((tt > ss) & (txs >= m) & (txs < 2 * m), lvl, 0.0)
            bit = (row & m) != 0
            before = jnp.where(bit, pltpu.roll(before, m, axis=0), before)
            last = jnp.where(bit, last, pltpu.roll(last, chunk - m, axis=0))
            m *= 2
        state_t = state_ref[h]
        o = jnp.dot(scores.astype(BF16), v, preferred_element_type=F32)
        o = o + lax.dot_general((q * jnp.exp(b)).astype(BF16), state_t.astype(BF16), nt,
                                preferred_element_type=F32)
        kd = (k * jnp.exp(last - b)).astype(BF16)
        state_ref[h] = state_t * jnp.exp(last[0:1, :]) + lax.dot_general(v, kd, tn, preferred_element_type=F32)

        o = o * lax.rsqrt(jnp.mean(o * o, axis=-1, keepdims=True) + LN_EPS) * on_ref[:, cols]
        gate = g_ref[:, cols]
        o_ref[:, cols] = (o * (gate * jax.nn.sigmoid(gate))).astype(o_ref.dtype)


def _hgrn(hcat, lb, onorm_g, batch, seq, d_a, chunk=128, heads=4):
    t = hcat.shape[0]
    d_b = lb.shape[0]
    n_heads = d_b // HGRN_HEAD_DIM
    heads = _tile(n_heads, heads)
    bw = heads * HGRN_HEAD_DIM
    chunk = _tile(seq, chunk)
    assert (2 * d_a) % bw == 0 and d_b % bw == 0
    n_chunks = seq // chunk
    base = (2 * d_a) // bw
    per = d_b // bw

    def col(which):
        return pl.BlockSpec((chunk, bw), lambda b, hb, c: (b * n_chunks + c, base + which * per + hb))

    vec = pl.BlockSpec((1, bw), lambda b, hb, c: (0, hb))
    return pl.pallas_call(
        functools.partial(_hgrn_kernel, heads=heads, chunk=chunk),
        grid=(batch, n_heads // heads, n_chunks),
        in_specs=[col(0), col(1), col(2), col(3), vec, vec],
        out_specs=pl.BlockSpec((chunk, bw), lambda b, hb, c: (b * n_chunks + c, hb)),
        out_shape=jax.ShapeDtypeStruct((t, d_b), BF16),
        scratch_shapes=[pltpu.VMEM((heads, HGRN_HEAD_DIM, HGRN_HEAD_DIM), F32)],
        compiler_params=_params(3),
        name="hgrn2",
    )(hcat, hcat, hcat, hcat, lb.reshape(1, d_b), onorm_g.reshape(1, d_b))


def _xattn_kernel(x_ref, wq_ref, k_ref, v_ref, o_ref, *, heads):
    q = jnp.dot(x_ref[...], wq_ref[...], preferred_element_type=F32).astype(BF16)
    d = q.shape[1]
    dh = d // heads
    nt = (((1,), (1,)), ((), ()))
    for h in range(heads):
        cols = slice(h * dh, (h + 1) * dh)
        s = lax.dot_general(q[:, cols], k_ref[:, cols], nt, preferred_element_type=F32) * (dh ** -0.5)
        s = s - jnp.max(s, axis=-1, keepdims=True)
        p = jnp.exp(s)
        p = p / jnp.sum(p, axis=-1, keepdims=True)
        o_ref[:, cols] = jnp.dot(p.astype(BF16), v_ref[:, cols], preferred_element_type=F32).astype(o_ref.dtype)


def _xattn(xb, w_q, kv, batch, seq, tm=512):
    t, d = xb.shape
    m_len = kv.shape[0] // batch
    tm = _tile(seq, tm)
    nt = seq // tm
    return pl.pallas_call(
        functools.partial(_xattn_kernel, heads=XA_HEADS),
        grid=(batch, nt),
        in_specs=[pl.BlockSpec((tm, d), lambda b, i: (b * nt + i, 0)),
                  pl.BlockSpec((d, d), lambda b, i: (0, 0)),
                  pl.BlockSpec((m_len, d), lambda b, i: (b, 0)),
                  pl.BlockSpec((m_len, d), lambda b, i: (b, 1))],
        out_specs=pl.BlockSpec((tm, d), lambda b, i: (b * nt + i, 0)),
        out_shape=jax.ShapeDtypeStruct((t, d), BF16),
        compiler_params=_params(2),
        name="xattn",
    )(xb, w_q, kv, kv)


def _compare_exchange(xs, i, j):
    a, b = xs[i], xs[j]
    xs[i], xs[j] = jnp.maximum(a, b), jnp.minimum(a, b)


def _bitonic_merge_desc(xs):
    n = len(xs)
    j = n // 2
    while j >= 1:
        for i in range(n):
            if i ^ j > i:
                _compare_exchange(xs, i, i ^ j)
        j //= 2


def _bitonic_sort_desc(xs):
    n = len(xs)
    k = 2
    while k <= n:
        j = k // 2
        while j >= 1:
            for i in range(n):
                l = i ^ j
                if l > i:
                    if i & k == 0 or k == n:
                        _compare_exchange(xs, i, l)
                    else:
                        _compare_exchange(xs, l, i)
            j //= 2
        k *= 2


def _top_merge_desc(xs, ys):
    n = len(xs)
    zs = [jnp.maximum(xs[r], ys[n - 1 - r]) for r in range(n)]
    _bitonic_merge_desc(zs)
    return zs


def _top16_rows(s):
    nk = s.shape[0]
    assert nk == 8 * PEER_TOPK
    xs = [s[8 * g:8 * (g + 1), :] for g in range(PEER_TOPK)]
    _bitonic_sort_desc(xs)
    for shift in (4, 2, 1):
        xs = _top_merge_desc(xs, [pltpu.roll(x, shift, axis=0) for x in xs])
    return xs


def _peer_route_kernel(x_ref, wq_ref, keys_ref, n1_ref, c1_ref, r2_ref, e2_ref):
    k = PEER_TOPK
    q = jnp.dot(x_ref[...], wq_ref[...], preferred_element_type=F32).astype(BF16)
    half = q.shape[1] // 2
    nt = (((1,), (1,)), ((), ()))
    s1 = lax.dot_general(keys_ref[0], q[:, :half], nt, preferred_element_type=F32)
    s2 = lax.dot_general(keys_ref[1], q[:, half:], nt, preferred_element_type=F32)
    a = _top16_rows(s1)
    b = _top16_rows(s2)
    first = [a[0] + b[j] for j in range(k)]
    rest = [a[i] + b[j] for i in range(1, k) for j in range(k) if (i + 1) * (j + 1) <= k]
    neg = jnp.full_like(a[0], -jnp.inf)
    rest = rest + [neg] * (3 * k - len(rest))
    lists = [rest[0:k], rest[k:2 * k], rest[2 * k:3 * k]]
    for l in lists:
        _bitonic_sort_desc(l)
    best = _top_merge_desc(_top_merge_desc(first, lists[0]), _top_merge_desc(lists[1], lists[2]))
    theta = best[k - 1]
    z = jnp.exp(best[0] - best[0])
    for r in range(1, k):
        z = z + jnp.exp(best[r] - best[0])
    inv_z = 1.0 / z

    nk, tm = s1.shape
    rep = lambda slab: jnp.broadcast_to(slab[0:1, :], (nk, tm))
    theta_f = rep(theta)
    n1 = jnp.zeros((nk, tm), F32)
    r2 = jnp.zeros((nk, tm), F32)
    for r in range(k):
        br = rep(b[r])
        n1 = n1 + jnp.where(s1 + br >= theta_f, 1.0, 0.0)
        r2 = r2 + jnp.where(s2 < br, 1.0, 0.0)
    n1_ref[0] = n1
    r2_ref[0] = r2
    c1_ref[0] = jnp.exp(s1 - rep(a[0])) * rep(inv_z)
    e2_ref[0] = jnp.exp(s2 - rep(b[0]))


def _peer_route(xb, w_q, keys, tm=512):
    t, d = xb.shape
    nk, half = keys.shape[1], keys.shape[2]
    tm = _tile(t, tm)
    out = jax.ShapeDtypeStruct((PEER_HEADS, nk, t), F32)
    ospec = pl.BlockSpec((1, nk, tm), lambda i, h: (h, 0, i))
    return pl.pallas_call(
        _peer_route_kernel,
        grid=(t // tm, PEER_HEADS),
        in_specs=[pl.BlockSpec((tm, d), lambda i, h: (i, 0)),
                  pl.BlockSpec((d, 2 * half), lambda i, h: (0, h)),
                  pl.BlockSpec((2, nk, half), lambda i, h: (0, 0, 0))],
        out_specs=[ospec, ospec, ospec, ospec],
        out_shape=[out, out, out, out],
        compiler_params=_params(2),
        name="peer_route",
    )(xb, w_q, keys)


def _peer_expert_kernel(x_ref, u_ref, vt_ref, n1_ref, c1_ref, r2_ref, e2_ref, o_ref, acc_ref, *, rows_per_block):
    eb = pl.program_id(1)

    @pl.when(eb == 0)
    def _():
        acc_ref[...] = jnp.zeros_like(acc_ref)

    nt = (((1,), (1,)), ((), ()))
    act = jax.nn.gelu(lax.dot_general(u_ref[...], x_ref[...], nt, preferred_element_type=F32))
    heads, nk, _ = r2_ref.shape
    row0 = (eb * rows_per_block) % n1_ref.shape[1]
    parts = []
    for il in range(rows_per_block):
        gate = None
        for h in range(heads):
            n = n1_ref[h, pl.ds(row0 + il, 1), :]
            c = c1_ref[h, pl.ds(row0 + il, 1), :]
            term = jnp.where(r2_ref[h] < n, e2_ref[h] * c, 0.0)
            gate = term if gate is None else gate + term
        parts.append((act[il * nk:(il + 1) * nk, :] * gate).astype(BF16))
    hid = jnp.concatenate(parts, axis=0)
    acc_ref[...] += jnp.dot(vt_ref[...], hid, preferred_element_type=F32)

    @pl.when(eb == pl.num_programs(1) - 1)
    def _():
        o_ref[...] = acc_ref[...].T


def _peer_experts(xb, u, vt, n1, c1, r2, e2, tm=512, te=512):
    t, d = xb.shape
    ne = u.shape[0]
    heads, nk, _ = n1.shape
    tm, te = _tile(t, tm), _tile(ne, te)
    rows = te // nk
    rblk = max(rows, 8)
    assert te % nk == 0 and rblk % rows == 0 and nk % rblk == 0
    per = rblk // rows
    row_spec = pl.BlockSpec((heads, rblk, tm), lambda i, e: (0, e // per, i))
    full_spec = pl.BlockSpec((heads, nk, tm), lambda i, e: (0, 0, i))
    return pl.pallas_call(
        functools.partial(_peer_expert_kernel, rows_per_block=rows),
        grid=(t // tm, ne // te),
        in_specs=[pl.BlockSpec((tm, d), lambda i, e: (i, 0)),
                  pl.BlockSpec((te, d), lambda i, e: (e, 0)),
                  pl.BlockSpec((d, te), lambda i, e: (0, e)),
                  row_spec, row_spec, full_spec, full_spec],
        out_specs=pl.BlockSpec((tm, d), lambda i, e: (i, 0)),
        out_shape=jax.ShapeDtypeStruct((t, d), F32),
        scratch_shapes=[pltpu.VMEM((d, tm), F32)],
        compiler_params=_params(2),
        name="peer_experts",
    )(xb, u, vt, n1, c1, r2, e2)


def kernel(x, mem, ev_w_in, ev_sgu_ln_g, ev_sgu_ln_b, ev_w_s, ev_b_s, ev_lb_logits, ev_onorm_g, ev_w_out, od_w_in, od_conv_w, od_w_out, mix_ln_g, mix_ln_b, xa_w_q, xa_w_kv, xa_w_o, xa_ln_g, xa_ln_b, peer_w_q, peer_keys, peer_u, peer_v, ffn_ln_g, ffn_ln_b):
    batch, seq, d = x.shape
    depth = mix_ln_g.shape[0]
    alpha = (2.0 * depth) ** 0.25
    t = batch * seq
    d_a = ev_sgu_ln_g.shape[1]

    p = jax.nn.softmax(ev_lb_logits.astype(F32), axis=0)
    lower_bounds = jnp.clip(jnp.cumsum(p, axis=0) - p[0], 0.0, 1.0)

    h = x.reshape(t, d)
    hb = h.astype(BF16)
    memb = mem.reshape(-1, d).astype(BF16)
    for layer in range(depth):
        j = layer // 2
        if layer % 2 == 0:
            hcat = _matmul(hb, ev_w_in[j].astype(BF16), F32)
            a_out = _sgu(hcat, ev_sgu_ln_g[j], ev_sgu_ln_b[j], ev_w_s[j], ev_b_s[j])
            b_out = _hgrn(hcat, lower_bounds[j], ev_onorm_g[j], batch, seq, d_a)
            w_out = ev_w_out[j].astype(BF16)
            us, ws = [a_out, b_out], [w_out[:d_a], w_out[d_a:]]
        else:
            us = [_shortconv(hb, od_w_in[j].astype(BF16), od_conv_w[j], seq)]
            ws = [od_w_out[j].astype(BF16)]
        h, hb = _proj_res_ln(us, ws, h, mix_ln_g[layer], mix_ln_b[layer], alpha)

        kv = _matmul(memb, xa_w_kv[layer].astype(BF16), BF16)
        att = _xattn(hb, xa_w_q[layer].astype(BF16), kv, batch, seq)
        h, hb = _proj_res_ln([att], [xa_w_o[layer].astype(BF16)], h, xa_ln_g[layer], xa_ln_b[layer], alpha)

        n1, c1, r2, e2 = _peer_route(hb, peer_w_q[layer].astype(BF16), peer_keys[layer].astype(BF16))
        y = _peer_experts(hb, peer_u[layer].astype(BF16), peer_v[layer].T.astype(BF16), n1, c1, r2, e2)
        h, hb = _res_ln(h, y, ffn_ln_g[layer], ffn_ln_b[layer], alpha)
    return h.reshape(batch, seq, d)
```

```python
import functools

import jax
import jax.numpy as jnp
from jax import lax
from jax.experimental import pallas as pl
from jax.experimental.pallas import tpu as pltpu

LN_EPS = 1e-5
F_FLOOR = 1e-30
HGRN_HEAD_DIM = 128
XA_HEADS = 4
PEER_HEADS = 8
PEER_TOPK = 16
V7X_VMEM_LIMIT_BYTES = 56 * 1024 * 1024
LANES = 128
F32_ROWS = 8
BF16_ROWS = 16
F32 = jnp.float32
BF16 = jnp.bfloat16


def _params(n_axes, flags=None):
    return pltpu.CompilerParams(dimension_semantics=("arbitrary",) * n_axes,
                                vmem_limit_bytes=V7X_VMEM_LIMIT_BYTES, flags=flags)


def _tile(n, want):
    t = min(n, want)
    while n % t:
        t -= LANES
    assert t > 0, (n, want)
    return t


def _layer_norm_rows(y, g, b):
    mu = jnp.mean(y, axis=-1, keepdims=True)
    yc = y - mu
    var = jnp.mean(yc * yc, axis=-1, keepdims=True)
    return yc * lax.rsqrt(var + LN_EPS) * g + b


def _mm_kernel(x_ref, w_ref, o_ref):
    o_ref[...] = jnp.dot(x_ref[...], w_ref[...], preferred_element_type=F32).astype(o_ref.dtype)


def _matmul(x, w, out_dtype, tm=1024, tn=1024):
    m, k = x.shape
    n = w.shape[1]
    tm, tn = _tile(m, tm), _tile(n, tn)
    return pl.pallas_call(
        _mm_kernel,
        grid=(n // tn, m // tm),
        in_specs=[pl.BlockSpec((tm, k), lambda j, i: (i, 0)),
                  pl.BlockSpec((k, tn), lambda j, i: (0, j))],
        out_specs=pl.BlockSpec((tm, tn), lambda j, i: (i, j)),
        out_shape=jax.ShapeDtypeStruct((m, n), out_dtype),
        compiler_params=_params(2),
        name="matmul",
    )(x, w)


def _proj_ln_kernel(*refs, n_in, alpha):
    us, ws = refs[:n_in], refs[n_in:2 * n_in]
    h_ref, g_ref, b_ref, of_ref, ob_ref = refs[2 * n_in:]
    acc = jnp.dot(us[0][...], ws[0][...], preferred_element_type=F32)
    for u_ref, w_ref in zip(us[1:], ws[1:]):
        acc = acc + jnp.dot(u_ref[...], w_ref[...], preferred_element_type=F32)
    out = _layer_norm_rows(alpha * h_ref[...] + acc, g_ref[...], b_ref[...])
    of_ref[...] = out
    ob_ref[...] = out.astype(BF16)


def _proj_res_ln(us, ws, h, g, b, alpha, tm=256):
    t, d = h.shape
    tm = _tile(t, tm)
    n_in = len(us)
    in_specs = [pl.BlockSpec((tm, u.shape[1]), lambda i: (i, 0)) for u in us]
    in_specs += [pl.BlockSpec(w.shape, lambda i: (0, 0)) for w in ws]
    in_specs += [pl.BlockSpec((tm, d), lambda i: (i, 0)),
                 pl.BlockSpec((1, d), lambda i: (0, 0)),
                 pl.BlockSpec((1, d), lambda i: (0, 0))]
    return pl.pallas_call(
        functools.partial(_proj_ln_kernel, n_in=n_in, alpha=alpha),
        grid=(t // tm,),
        in_specs=in_specs,
        out_specs=[pl.BlockSpec((tm, d), lambda i: (i, 0)), pl.BlockSpec((tm, d), lambda i: (i, 0))],
        out_shape=[jax.ShapeDtypeStruct((t, d), F32), jax.ShapeDtypeStruct((t, d), BF16)],
        compiler_params=_params(1),
        name="proj_res_ln",
    )(*us, *ws, h, g.reshape(1, d), b.reshape(1, d))


def _res_ln_kernel(h_ref, y_ref, g_ref, b_ref, of_ref, ob_ref, *, alpha):
    out = _layer_norm_rows(alpha * h_ref[...] + y_ref[...], g_ref[...], b_ref[...])
    of_ref[...] = out
    ob_ref[...] = out.astype(BF16)


def _res_ln(h, y, g, b, alpha, tm=256):
    t, d = h.shape
    tm = _tile(t, tm)
    row = pl.BlockSpec((tm, d), lambda i: (i, 0))
    vec = pl.BlockSpec((1, d), lambda i: (0, 0))
    return pl.pallas_call(
        functools.partial(_res_ln_kernel, alpha=alpha),
        grid=(t // tm,),
        in_specs=[row, row, vec, vec],
        out_specs=[row, row],
        out_shape=[jax.ShapeDtypeStruct((t, d), F32), jax.ShapeDtypeStruct((t, d), BF16)],
        compiler_params=_params(1),
        name="res_ln",
    )(h, y, g.reshape(1, d), b.reshape(1, d))


def _shortconv_kernel(x_ref, wb_ref, wc_ref, wx_ref, cw_ref, o_ref, carry_ref, *, tiles_per_seq):
    i = pl.program_id(1)

    @pl.when(i % tiles_per_seq == 0)
    def _():
        carry_ref[...] = jnp.zeros_like(carry_ref)

    x = x_ref[...]
    gate_b = jnp.dot(x, wb_ref[...], preferred_element_type=F32)
    gate_c = jnp.dot(x, wc_ref[...], preferred_element_type=F32)
    xin = jnp.dot(x, wx_ref[...], preferred_element_type=F32)
    z = gate_c * xin
    tm = z.shape[0]
    prev = carry_ref[...]
    p1, p2 = prev[7:8, :], prev[6:7, :]
    row = lax.broadcasted_iota(jnp.int32, z.shape, 0)
    z1 = jnp.where(row == 0, p1, pltpu.roll(z, 1, axis=0))
    z2 = jnp.where(row == 0, p2, jnp.where(row == 1, p1, pltpu.roll(z, 2, axis=0)))
    cw = cw_ref[...]
    y = cw[0:1, :] * z2 + cw[1:2, :] * z1 + cw[2:3, :] * z
    o_ref[...] = (gate_b * y).astype(o_ref.dtype)
    carry_ref[...] = z[tm - 8:tm, :]


def _shortconv(xb, w_in, conv_w, seq, tm=512, tc=512):
    t, d = xb.shape
    tm, tc = _tile(seq, tm), _tile(d, tc)
    nc = d // tc
    cw = jnp.zeros((8, d), F32).at[:conv_w.shape[0]].set(conv_w)
    wspec = lambda off: pl.BlockSpec((d, tc), lambda c, i: (0, off + c))
    return pl.pallas_call(
        functools.partial(_shortconv_kernel, tiles_per_seq=seq // tm),
        grid=(nc, t // tm),
        in_specs=[pl.BlockSpec((tm, d), lambda c, i: (i, 0)),
                  wspec(0), wspec(nc), wspec(2 * nc),
                  pl.BlockSpec((8, tc), lambda c, i: (0, c))],
        out_specs=pl.BlockSpec((tm, tc), lambda c, i: (i, c)),
        out_shape=jax.ShapeDtypeStruct((t, d), BF16),
        scratch_shapes=[pltpu.VMEM((8, tc), F32)],
        compiler_params=_params(2),
        name="shortconv",
    )(xb, w_in, w_in, w_in, cw)


def _sgu_kernel(au_ref, av_ref, g_ref, b_ref, ws_ref, bias_ref, o_ref, *, groups, chunk):
    u = jax.nn.gelu(au_ref[...])
    v = _layer_norm_rows(jax.nn.gelu(av_ref[...]), g_ref[...], b_ref[...]).astype(BF16)
    tm, da = u.shape
    gd = da // groups
    r = lax.broadcasted_iota(jnp.int32, (chunk, chunk), 0)
    c = lax.broadcasted_iota(jnp.int32, (chunk, chunk), 1)
    causal = r >= c
    for g in range(groups):
        w = jnp.where(causal, ws_ref[g], 0.0).astype(BF16)
        cols = slice(g * gd, (g + 1) * gd)
        for n in range(tm // chunk):
            rows = slice(n * chunk, (n + 1) * chunk)
            mixed = jnp.dot(w, v[rows, cols], preferred_element_type=F32) + bias_ref[:, cols]
            o_ref[rows, cols] = (u[rows, cols] * mixed).astype(o_ref.dtype)


def _sgu(hcat, ln_g, ln_b, w_s, b_s, tm=256):
    t = hcat.shape[0]
    groups, chunk, _ = w_s.shape
    da = ln_g.shape[0]
    gd = da // groups
    tm = _tile(t, tm)
    assert tm % chunk == 0
    bias = jnp.repeat(b_s.T, gd, axis=1)
    return pl.pallas_call(
        functools.partial(_sgu_kernel, groups=groups, chunk=chunk),
        grid=(t // tm,),
        in_specs=[pl.BlockSpec((tm, da), lambda i: (i, 0)),
                  pl.BlockSpec((tm, da), lambda i: (i, 1)),
                  pl.BlockSpec((1, da), lambda i: (0, 0)),
                  pl.BlockSpec((1, da), lambda i: (0, 0)),
                  pl.BlockSpec((groups, chunk, chunk), lambda i: (0, 0, 0)),
                  pl.BlockSpec((chunk, da), lambda i: (0, 0))],
        out_specs=pl.BlockSpec((tm, da), lambda i: (i, 0)),
        out_shape=jax.ShapeDtypeStruct((t, da), BF16),
        compiler_params=_params(1),
        name="sgu",
    )(hcat, hcat, ln_g.reshape(1, da), ln_b.reshape(1, da), w_s, bias)


def _hgrn_kernel(q_ref, f_ref, i_ref, g_ref, lb_ref, on_ref, o_ref, state_ref, *, heads, chunk):
    dh = HGRN_HEAD_DIM

    @pl.when(pl.program_id(2) == 0)
    def _():
        state_ref[...] = jnp.zeros_like(state_ref)

    row = lax.broadcasted_iota(jnp.int32, (chunk, dh), 0)
    tt = lax.broadcasted_iota(jnp.int32, (chunk, chunk), 0)
    ss = lax.broadcasted_iota(jnp.int32, (chunk, chunk), 1)
    txs = tt ^ ss
    nt = (((1,), (1,)), ((), ()))
    tn = (((0,), (0,)), ((), ()))

    for h in range(heads):
        cols = slice(h * dh, (h + 1) * dh)
        z = f_ref[:, cols]
        lb = lb_ref[:, cols]
        sig = jax.nn.sigmoid(z)
        f = lb + (1.0 - lb) * sig
        logf = jnp.log(jnp.maximum(f, F_FLOOR))
        k = (1.0 - lb) * jax.nn.sigmoid(-z)
        q = q_ref[:, cols]
        q = q * jax.nn.sigmoid(q)
        v = i_ref[:, cols].astype(BF16)

        b = logf
        sh = 1
        while sh < chunk:
            b = b + jnp.where(row >= sh, pltpu.roll(b, sh, axis=0), 0.0)
            sh *= 2

        scores = jnp.where(tt == ss, lax.dot_general(q.astype(BF16), k.astype(BF16), nt,
                                                     preferred_element_type=F32), 0.0)
        before = jnp.where(row >= 1, pltpu.roll(b, 1, axis=0), 0.0)
        last = b
        m = 1
        while m < chunk:
            qd = (q * jnp.exp(b - before)).astype(BF16)
            kd = (k * jnp.exp(last - b)).astype(BF16)
            lvl = lax.dot_general(qd, kd, nt, preferred_element_type=F32)
            scores = scores + jnp.where((tt > ss) & (txs >= m) & (txs < 2 * m), lvl, 0.0)
            bit = (row & m) != 0
            before = jnp.where(bit, pltpu.roll(before, m, axis=0), before)
            last = jnp.where(bit, last, pltpu.roll(last, chunk - m, axis=0))
            m *= 2
        state_t = state_ref[h]
        o = jnp.dot(scores.astype(BF16), v, preferred_element_type=F32)
        o = o + lax.dot_general((q * jnp.exp(b)).astype(BF16), state_t.astype(BF16), nt,
                                preferred_element_type=F32)
        kd = (k * jnp.exp(last - b)).astype(BF16)
        state_ref[h] = state_t * jnp.exp(last[0:1, :]) + lax.dot_general(v, kd, tn, preferred_element_type=F32)

        o = o * lax.rsqrt(jnp.mean(o * o, axis=-1, keepdims=True) + LN_EPS) * on_ref[:, cols]
        gate = g_ref[:, cols]
        o_ref[:, cols] = (o * (gate * jax.nn.sigmoid(gate))).astype(o_ref.dtype)


def _hgrn(hcat, lb, onorm_g, batch, seq, d_a, chunk=128, heads=4):
    t = hcat.shape[0]
    d_b = lb.shape[0]
    n_heads = d_b // HGRN_HEAD_DIM
    heads = _tile(n_heads, heads)
    bw = heads * HGRN_HEAD_DIM
    chunk = _tile(seq, chunk)
    assert (2 * d_a) % bw == 0 and d_b % bw == 0
    n_chunks = seq // chunk
    base = (2 * d_a) // bw
    per = d_b // bw

    def col(which):
        return pl.BlockSpec((chunk, bw), lambda b, hb, c: (b * n_chunks + c, base + which * per + hb))

    vec = pl.BlockSpec((1, bw), lambda b, hb, c: (0, hb))
    return pl.pallas_call(
        functools.partial(_hgrn_kernel, heads=heads, chunk=chunk),
        grid=(batch, n_heads // heads, n_chunks),
        in_specs=[col(0), col(1), col(2), col(3), vec, vec],
        out_specs=pl.BlockSpec((chunk, bw), lambda b, hb, c: (b * n_chunks + c, hb)),
        out_shape=jax.ShapeDtypeStruct((t, d_b), BF16),
        scratch_shapes=[pltpu.VMEM((heads, HGRN_HEAD_DIM, HGRN_HEAD_DIM), F32)],
        compiler_params=_params(3),
        name="hgrn2",
    )(hcat, hcat, hcat, hcat, lb.reshape(1, d_b), onorm_g.reshape(1, d_b))


def _xattn_kernel(x_ref, wq_ref, k_ref, v_ref, o_ref, *, heads):
    q = jnp.dot(x_ref[...], wq_ref[...], preferred_element_type=F32).astype(BF16)
    d = q.shape[1]
    dh = d // heads
    nt = (((1,), (1,)), ((), ()))
    for h in range(heads):
        cols = slice(h * dh, (h + 1) * dh)
        s = lax.dot_general(q[:, cols], k_ref[:, cols], nt, preferred_element_type=F32) * (dh ** -0.5)
        s = s - jnp.max(s, axis=-1, keepdims=True)
        p = jnp.exp(s)
        p = p / jnp.sum(p, axis=-1, keepdims=True)
        o_ref[:, cols] = jnp.dot(p.astype(BF16), v_ref[:, cols], preferred_element_type=F32).astype(o_ref.dtype)


def _xattn(xb, w_q, kv, batch, seq, tm=512):
    t, d = xb.shape
    m_len = kv.shape[0] // batch
    tm = _tile(seq, tm)
    nt = seq // tm
    return pl.pallas_call(
        functools.partial(_xattn_kernel, heads=XA_HEADS),
        grid=(batch, nt),
        in_specs=[pl.BlockSpec((tm, d), lambda b, i: (b * nt + i, 0)),
                  pl.BlockSpec((d, d), lambda b, i: (0, 0)),
                  pl.BlockSpec((m_len, d), lambda b, i: (b, 0)),
                  pl.BlockSpec((m_len, d), lambda b, i: (b, 1))],
        out_specs=pl.BlockSpec((tm, d), lambda b, i: (b * nt + i, 0)),
        out_shape=jax.ShapeDtypeStruct((t, d), BF16),
        compiler_params=_params(2),
        name="xattn",
    )(xb, w_q, kv, kv)


def _compare_exchange(xs, i, j):
    a, b = xs[i], xs[j]
    xs[i], xs[j] = jnp.maximum(a, b), jnp.minimum(a, b)


def _bitonic_merge_desc(xs):
    n = len(xs)
    j = n // 2
    while j >= 1:
        for i in range(n):
            if i ^ j > i:
                _compare_exchange(xs, i, i ^ j)
        j //= 2


def _bitonic_sort_desc(xs):
    n = len(xs)
    k = 2
    while k <= n:
        j = k // 2
        while j >= 1:
            for i in range(n):
                l = i ^ j
                if l > i:
                    if i & k == 0 or k == n:
                        _compare_exchange(xs, i, l)
                    else:
                        _compare_exchange(xs, l, i)
            j //= 2
        k *= 2


def _top_merge_desc(xs, ys):
    n = len(xs)
    zs = [jnp.maximum(xs[r], ys[n - 1 - r]) for r in range(n)]
    _bitonic_merge_desc(zs)
    return zs


def _top16_rows(s):
    nk = s.shape[0]
    assert nk == 8 * PEER_TOPK
    xs = [s[8 * g:8 * (g + 1), :] for g in range(PEER_TOPK)]
    _bitonic_sort_desc(xs)
    for shift in (4, 2, 1):
        xs = _top_merge_desc(xs, [pltpu.roll(x, shift, axis=0) for x in xs])
    return xs


def _count_true_prefix(pred, ts):
    t8 = pred(ts[7])
    t4 = pred(jnp.where(t8, ts[11], ts[3]))
    t2 = pred(jnp.where(t8, jnp.where(t4, ts[13], ts[9]), jnp.where(t4, ts[5], ts[1])))
    hi = jnp.where(t4, jnp.where(t2, ts[14], ts[12]), jnp.where(t2, ts[10], ts[8]))
    lo = jnp.where(t4, jnp.where(t2, ts[6], ts[4]), jnp.where(t2, ts[2], ts[0]))
    t1 = pred(jnp.where(t8, hi, lo))
    n = (jnp.where(t8, 8.0, 0.0) + jnp.where(t4, 4.0, 0.0)) + (jnp.where(t2, 2.0, 0.0) + jnp.where(t1, 1.0, 0.0))
    return jnp.where(pred(ts[15]), 16.0, n)


def _peer_route_kernel(x_ref, wq_ref, keys_ref, n1_ref, c1_ref, r2_ref, e2_ref, q_ref):
    k = PEER_TOPK
    q_ref[...] = jnp.dot(x_ref[...], wq_ref[...], preferred_element_type=F32).astype(q_ref.dtype)
    half = q_ref.shape[1] // 2
    nk = keys_ref.shape[1]
    nt = (((1,), (1,)), ((), ()))

    def lane_tile(lt, carry):
        q = q_ref[pl.ds(pl.multiple_of(lt * LANES, LANES), LANES), :]
        s1 = lax.dot_general(keys_ref[0], q[:, :half], nt, preferred_element_type=F32)
        s2 = lax.dot_general(keys_ref[1], q[:, half:], nt, preferred_element_type=F32)
        a = _top16_rows(s1)
        b = _top16_rows(s2)
        first = [a[0] + b[j] for j in range(k)]
        rest = [a[i] + b[j] for i in range(1, k) for j in range(k) if (i + 1) * (j + 1) <= k]
        neg = jnp.full_like(a[0], -jnp.inf)
        rest = rest + [neg] * (3 * k - len(rest))
        lists = [rest[0:k], rest[k:2 * k], rest[2 * k:3 * k]]
        for l in lists:
            _bitonic_sort_desc(l)
        best = _top_merge_desc(_top_merge_desc(first, lists[0]), _top_merge_desc(lists[1], lists[2]))
        z = jnp.exp(best[0] - best[0])
        for r in range(1, k):
            z = z + jnp.exp(best[r] - best[0])

        rep = lambda slab: jnp.broadcast_to(slab[0:1, :], (nk, LANES))
        theta = rep(best[k - 1])
        bs = [rep(x) for x in b]
        n1_ref[0, lt] = _count_true_prefix(lambda br: s1 + br >= theta, bs)
        r2_ref[0, lt] = _count_true_prefix(lambda br: s2 < br, bs).astype(r2_ref.dtype)
        c1_ref[0, lt] = jnp.exp(s1 - rep(a[0])) * rep(1.0 / z)
        e2_ref[0, lt] = jnp.exp(s2 - bs[0]).astype(e2_ref.dtype)
        return carry

    lax.fori_loop(0, q_ref.shape[0] // LANES, lane_tile, 0)


def _peer_route(xb, w_q, keys, tm=512):
    t, d = xb.shape
    nk, half = keys.shape[1], keys.shape[2]
    tm = _tile(t, tm)
    assert tm % LANES == 0
    out = jax.ShapeDtypeStruct((PEER_HEADS, t // LANES, nk, LANES), F32)
    out_b = jax.ShapeDtypeStruct((PEER_HEADS, t // LANES, nk, LANES), BF16)
    ospec = pl.BlockSpec((1, tm // LANES, nk, LANES), lambda i, h: (h, i, 0, 0))
    return pl.pallas_call(
        _peer_route_kernel,
        grid=(t // tm, PEER_HEADS),
        in_specs=[pl.BlockSpec((tm, d), lambda i, h: (i, 0)),
                  pl.BlockSpec((d, 2 * half), lambda i, h: (0, h)),
                  pl.BlockSpec((2, nk, half), lambda i, h: (0, 0, 0))],
        out_specs=[ospec, ospec, ospec, ospec],
        out_shape=[out, out, out_b, out_b],
        scratch_shapes=[pltpu.VMEM((tm, 2 * half), BF16)],
        compiler_params=_params(2),
        name="peer_route",
    )(xb, w_q, keys)


def _gated_activation(act_ref, hid_ref, n1_ref, c1_ref, r2_ref, e2_ref):
    heads, n_tiles, nk, _ = r2_ref.shape
    zero = jnp.zeros((BF16_ROWS, LANES), BF16)
    group = 4
    for lt in range(n_tiles):
        lanes = slice(lt * LANES, (lt + 1) * LANES)
        for i in range(n1_ref.shape[2]):
            for j0 in range(0, nk // BF16_ROWS, group):
                gates = [None] * group
                for h in range(heads):
                    nb = jnp.broadcast_to(n1_ref[h, lt, i:i + 1, :], (BF16_ROWS, LANES)).astype(BF16)
                    cb = jnp.broadcast_to(c1_ref[h, lt, i:i + 1, :], (BF16_ROWS, LANES)).astype(BF16)
                    for g in range(group):
                        rows = slice((j0 + g) * BF16_ROWS, (j0 + g + 1) * BF16_ROWS)
                        term = jnp.where(r2_ref[h, lt, rows, :] < nb, e2_ref[h, lt, rows, :] * cb, zero)
                        gates[g] = term if gates[g] is None else gates[g] + term
                for g in range(group):
                    erows = slice(i * nk + (j0 + g) * BF16_ROWS, i * nk + (j0 + g + 1) * BF16_ROWS)
                    hid_ref[erows, lanes] = jax.nn.gelu(act_ref[erows, lanes]).astype(BF16) * gates[g]


def _peer_expert_kernel(xt_ref, u_ref, vt_ref, n1_ref, c1_ref, r2_ref, e2_ref, o_ref, acc_ref, act_ref, hid_ref):
    k = pl.program_id(1)

    @pl.when(k == 0)
    def _():
        acc_ref[...] = jnp.zeros_like(acc_ref)

    act_ref[...] = jnp.dot(u_ref[...], xt_ref[...], preferred_element_type=F32)
    _gated_activation(act_ref, hid_ref, n1_ref, c1_ref, r2_ref, e2_ref)
    acc_ref[...] += jnp.dot(vt_ref[...], hid_ref[...], preferred_element_type=F32)

    @pl.when(k == pl.num_programs(1) - 1)
    def _():
        o_ref[...] = acc_ref[...].T


def _peer_experts(xt, u, vt, n1, c1, r2, e2, tm=512):
    d, t = xt.shape
    ne = u.shape[0]
    heads, _, nk, _ = n1.shape
    tm = _tile(t, tm)
    rows = F32_ROWS
    te = rows * nk
    assert ne % te == 0 and nk % BF16_ROWS == 0 and tm % LANES == 0
    full_spec = pl.BlockSpec((heads, tm // LANES, nk, LANES), lambda i, k: (0, i, 0, 0))
    row_spec = pl.BlockSpec((heads, tm // LANES, rows, LANES), lambda i, k: (0, i, k, 0))
    return pl.pallas_call(
        _peer_expert_kernel,
        grid=(t // tm, ne // te),
        in_specs=[pl.BlockSpec((d, tm), lambda i, k: (0, i)),
                  pl.BlockSpec((te, d), lambda i, k: (k, 0)),
                  pl.BlockSpec((d, te), lambda i, k: (0, k)),
                  row_spec, row_spec, full_spec, full_spec],
        out_specs=pl.BlockSpec((tm, d), lambda i, k: (i, 0)),
        out_shape=jax.ShapeDtypeStruct((t, d), F32),
        scratch_shapes=[pltpu.VMEM((d, tm), F32), pltpu.VMEM((te, tm), F32), pltpu.VMEM((te, tm), BF16)],
        compiler_params=_params(2),
        name="peer_experts",
    )(xt, u, vt, n1, c1, r2, e2)


def kernel(x, mem, ev_w_in, ev_sgu_ln_g, ev_sgu_ln_b, ev_w_s, ev_b_s, ev_lb_logits, ev_onorm_g, ev_w_out, od_w_in, od_conv_w, od_w_out, mix_ln_g, mix_ln_b, xa_w_q, xa_w_kv, xa_w_o, xa_ln_g, xa_ln_b, peer_w_q, peer_keys, peer_u, peer_v, ffn_ln_g, ffn_ln_b):
    batch, seq, d = x.shape
    depth = mix_ln_g.shape[0]
    alpha = (2.0 * depth) ** 0.25
    t = batch * seq
    d_a = ev_sgu_ln_g.shape[1]

    p = jax.nn.softmax(ev_lb_logits.astype(F32), axis=0)
    lower_bounds = jnp.clip(jnp.cumsum(p, axis=0) - p[0], 0.0, 1.0)

    h = x.reshape(t, d)
    hb = h.astype(BF16)
    memb = mem.reshape(-1, d).astype(BF16)
    for layer in range(depth):
        j = layer // 2
        if layer % 2 == 0:
            hcat = _matmul(hb, ev_w_in[j].astype(BF16), F32)
            a_out = _sgu(hcat, ev_sgu_ln_g[j], ev_sgu_ln_b[j], ev_w_s[j], ev_b_s[j])
            b_out = _hgrn(hcat, lower_bounds[j], ev_onorm_g[j], batch, seq, d_a)
            w_out = ev_w_out[j].astype(BF16)
            us, ws = [a_out, b_out], [w_out[:d_a], w_out[d_a:]]
        else:
            us = [_shortconv(hb, od_w_in[j].astype(BF16), od_conv_w[j], seq)]
            ws = [od_w_out[j].astype(BF16)]
        h, hb = _proj_res_ln(us, ws, h, mix_ln_g[layer], mix_ln_b[layer], alpha)

        kv = _matmul(memb, xa_w_kv[layer].astype(BF16), BF16)
        att = _xattn(hb, xa_w_q[layer].astype(BF16), kv, batch, seq)
        h, hb = _proj_res_ln([att], [xa_w_o[layer].astype(BF16)], h, xa_ln_g[layer], xa_ln_b[layer], alpha)

        n1, c1, r2, e2 = _peer_route(hb, peer_w_q[layer].astype(BF16), peer_keys[layer].astype(BF16))
        y = _peer_experts(hb.T, peer_u[layer].astype(BF16), peer_v[layer].T.astype(BF16), n1, c1, r2, e2)
        h, hb = _res_ln(h, y, ffn_ln_g[layer], ffn_ln_b[layer], alpha)
    return h.reshape(batch, seq, d)
```

```python
import functools

import jax
import jax.numpy as jnp
from jax import lax
from jax.experimental import pallas as pl
from jax.experimental.pallas import tpu as pltpu

LN_EPS = 1e-5
F_FLOOR = 1e-30
HGRN_HEAD_DIM = 128
XA_HEADS = 4
PEER_HEADS = 8
PEER_TOPK = 16
V7X_VMEM_LIMIT_BYTES = 56 * 1024 * 1024
LANES = 128
F32_ROWS = 8
BF16_ROWS = 16
F32 = jnp.float32
BF16 = jnp.bfloat16


def _params(n_axes, flags=None):
    return pltpu.CompilerParams(dimension_semantics=("arbitrary",) * n_axes,
                                vmem_limit_bytes=V7X_VMEM_LIMIT_BYTES, flags=flags)


def _tile(n, want):
    t = min(n, want)
    while n % t:
        t -= LANES
    assert t > 0, (n, want)
    return t


def _layer_norm_rows(y, g, b):
    mu = jnp.mean(y, axis=-1, keepdims=True)
    yc = y - mu
    var = jnp.mean(yc * yc, axis=-1, keepdims=True)
    return yc * lax.rsqrt(var + LN_EPS) * g + b


def _mm_kernel(x_ref, w_ref, o_ref):
    o_ref[...] = jnp.dot(x_ref[...], w_ref[...], preferred_element_type=F32).astype(o_ref.dtype)


def _matmul(x, w, out_dtype, tm=1024, tn=1024):
    m, k = x.shape
    n = w.shape[1]
    tm, tn = _tile(m, tm), _tile(n, tn)
    return pl.pallas_call(
        _mm_kernel,
        grid=(n // tn, m // tm),
        in_specs=[pl.BlockSpec((tm, k), lambda j, i: (i, 0)),
                  pl.BlockSpec((k, tn), lambda j, i: (0, j))],
        out_specs=pl.BlockSpec((tm, tn), lambda j, i: (i, j)),
        out_shape=jax.ShapeDtypeStruct((m, n), out_dtype),
        compiler_params=_params(2),
        name="matmul",
    )(x, w)


def _proj_ln_kernel(*refs, n_in, alpha):
    us, ws = refs[:n_in], refs[n_in:2 * n_in]
    h_ref, g_ref, b_ref, of_ref, ob_ref = refs[2 * n_in:]
    acc = jnp.dot(us[0][...], ws[0][...], preferred_element_type=F32)
    for u_ref, w_ref in zip(us[1:], ws[1:]):
        acc = acc + jnp.dot(u_ref[...], w_ref[...], preferred_element_type=F32)
    out = _layer_norm_rows(alpha * h_ref[...] + acc, g_ref[...], b_ref[...])
    of_ref[...] = out
    ob_ref[...] = out.astype(BF16)


def _proj_res_ln(us, ws, h, g, b, alpha, tm=256):
    t, d = h.shape
    tm = _tile(t, tm)
    n_in = len(us)
    in_specs = [pl.BlockSpec((tm, u.shape[1]), lambda i: (i, 0)) for u in us]
    in_specs += [pl.BlockSpec(w.shape, lambda i: (0, 0)) for w in ws]
    in_specs += [pl.BlockSpec((tm, d), lambda i: (i, 0)),
                 pl.BlockSpec((1, d), lambda i: (0, 0)),
                 pl.BlockSpec((1, d), lambda i: (0, 0))]
    return pl.pallas_call(
        functools.partial(_proj_ln_kernel, n_in=n_in, alpha=alpha),
        grid=(t // tm,),
        in_specs=in_specs,
        out_specs=[pl.BlockSpec((tm, d), lambda i: (i, 0)), pl.BlockSpec((tm, d), lambda i: (i, 0))],
        out_shape=[jax.ShapeDtypeStruct((t, d), F32), jax.ShapeDtypeStruct((t, d), BF16)],
        compiler_params=_params(1),
        name="proj_res_ln",
    )(*us, *ws, h, g.reshape(1, d), b.reshape(1, d))


def _shortconv_kernel(x_ref, wb_ref, wc_ref, wx_ref, cw_ref, o_ref, carry_ref, *, tiles_per_seq):
    i = pl.program_id(1)

    @pl.when(i % tiles_per_seq == 0)
    def _():
        carry_ref[...] = jnp.zeros_like(carry_ref)

    x = x_ref[...]
    gate_b = jnp.dot(x, wb_ref[...], preferred_element_type=F32)
    gate_c = jnp.dot(x, wc_ref[...], preferred_element_type=F32)
    xin = jnp.dot(x, wx_ref[...], preferred_element_type=F32)
    z = gate_c * xin
    tm = z.shape[0]
    prev = carry_ref[...]
    p1, p2 = prev[7:8, :], prev[6:7, :]
    row = lax.broadcasted_iota(jnp.int32, z.shape, 0)
    z1 = jnp.where(row == 0, p1, pltpu.roll(z, 1, axis=0))
    z2 = jnp.where(row == 0, p2, jnp.where(row == 1, p1, pltpu.roll(z, 2, axis=0)))
    cw = cw_ref[...]
    y = cw[0:1, :] * z2 + cw[1:2, :] * z1 + cw[2:3, :] * z
    o_ref[...] = (gate_b * y).astype(o_ref.dtype)
    carry_ref[...] = z[tm - 8:tm, :]


def _shortconv(xb, w_in, conv_w, seq, tm=512, tc=512):
    t, d = xb.shape
    tm, tc = _tile(seq, tm), _tile(d, tc)
    nc = d // tc
    cw = jnp.zeros((8, d), F32).at[:conv_w.shape[0]].set(conv_w)
    wspec = lambda off: pl.BlockSpec((d, tc), lambda c, i: (0, off + c))
    return pl.pallas_call(
        functools.partial(_shortconv_kernel, tiles_per_seq=seq // tm),
        grid=(nc, t // tm),
        in_specs=[pl.BlockSpec((tm, d), lambda c, i: (i, 0)),
                  wspec(0), wspec(nc), wspec(2 * nc),
                  pl.BlockSpec((8, tc), lambda c, i: (0, c))],
        out_specs=pl.BlockSpec((tm, tc), lambda c, i: (i, c)),
        out_shape=jax.ShapeDtypeStruct((t, d), BF16),
        scratch_shapes=[pltpu.VMEM((8, tc), F32)],
        compiler_params=_params(2),
        name="shortconv",
    )(xb, w_in, w_in, w_in, cw)


def _sgu_kernel(au_ref, av_ref, g_ref, b_ref, ws_ref, bias_ref, o_ref, *, groups, chunk):
    u = jax.nn.gelu(au_ref[...])
    v = _layer_norm_rows(jax.nn.gelu(av_ref[...]), g_ref[...], b_ref[...]).astype(BF16)
    tm, da = u.shape
    gd = da // groups
    r = lax.broadcasted_iota(jnp.int32, (chunk, chunk), 0)
    c = lax.broadcasted_iota(jnp.int32, (chunk, chunk), 1)
    causal = r >= c
    for g in range(groups):
        w = jnp.where(causal, ws_ref[g], 0.0).astype(BF16)
        cols = slice(g * gd, (g + 1) * gd)
        for n in range(tm // chunk):
            rows = slice(n * chunk, (n + 1) * chunk)
            mixed = jnp.dot(w, v[rows, cols], preferred_element_type=F32) + bias_ref[:, cols]
            o_ref[rows, cols] = (u[rows, cols] * mixed).astype(o_ref.dtype)


def _sgu(hcat, ln_g, ln_b, w_s, b_s, tm=256):
    t = hcat.shape[0]
    groups, chunk, _ = w_s.shape
    da = ln_g.shape[0]
    gd = da // groups
    tm = _tile(t, tm)
    assert tm % chunk == 0
    bias = jnp.repeat(b_s.T, gd, axis=1)
    return pl.pallas_call(
        functools.partial(_sgu_kernel, groups=groups, chunk=chunk),
        grid=(t // tm,),
        in_specs=[pl.BlockSpec((tm, da), lambda i: (i, 0)),
                  pl.BlockSpec((tm, da), lambda i: (i, 1)),
                  pl.BlockSpec((1, da), lambda i: (0, 0)),
                  pl.BlockSpec((1, da), lambda i: (0, 0)),
                  pl.BlockSpec((groups, chunk, chunk), lambda i: (0, 0, 0)),
                  pl.BlockSpec((chunk, da), lambda i: (0, 0))],
        out_specs=pl.BlockSpec((tm, da), lambda i: (i, 0)),
        out_shape=jax.ShapeDtypeStruct((t, da), BF16),
        compiler_params=_params(1),
        name="sgu",
    )(hcat, hcat, ln_g.reshape(1, da), ln_b.reshape(1, da), w_s, bias)


def _hgrn_kernel(q_ref, f_ref, i_ref, g_ref, lb_ref, on_ref, o_ref, state_ref, *, heads, chunk):
    dh = HGRN_HEAD_DIM

    @pl.when(pl.program_id(2) == 0)
    def _():
        state_ref[...] = jnp.zeros_like(state_ref)

    row = lax.broadcasted_iota(jnp.int32, (chunk, dh), 0)
    tt = lax.broadcasted_iota(jnp.int32, (chunk, chunk), 0)
    ss = lax.broadcasted_iota(jnp.int32, (chunk, chunk), 1)
    txs = tt ^ ss
    nt = (((1,), (1,)), ((), ()))
    tn = (((0,), (0,)), ((), ()))

    for h in range(heads):
        cols = slice(h * dh, (h + 1) * dh)
        z = f_ref[:, cols]
        lb = lb_ref[:, cols]
        sig = jax.nn.sigmoid(z)
        f = lb + (1.0 - lb) * sig
        logf = jnp.log(jnp.maximum(f, F_FLOOR))
        k = (1.0 - lb) * jax.nn.sigmoid(-z)
        q = q_ref[:, cols]
        q = q * jax.nn.sigmoid(q)
        v = i_ref[:, cols].astype(BF16)

        b = logf
        sh = 1
        while sh < chunk:
            b = b + jnp.where(row >= sh, pltpu.roll(b, sh, axis=0), 0.0)
            sh *= 2

        scores = jnp.where(tt == ss, lax.dot_general(q.astype(BF16), k.astype(BF16), nt,
                                                     preferred_element_type=F32), 0.0)
        before = jnp.where(row >= 1, pltpu.roll(b, 1, axis=0), 0.0)
        last = b
        m = 1
        while m < chunk:
            qd = (q * jnp.exp(b - before)).astype(BF16)
            kd = (k * jnp.exp(last - b)).astype(BF16)
            lvl = lax.dot_general(qd, kd, nt, preferred_element_type=F32)
            scores = scores + jnp.where((tt > ss) & (txs >= m) & (txs < 2 * m), lvl, 0.0)
            bit = (row & m) != 0
            before = jnp.where(bit, pltpu.roll(before, m, axis=0), before)
            last = jnp.where(bit, last, pltpu.roll(last, chunk - m, axis=0))
            m *= 2
        state_t = state_ref[h]
        o = jnp.dot(scores.astype(BF16), v, preferred_element_type=F32)
        o = o + lax.dot_general((q * jnp.exp(b)).astype(BF16), state_t.astype(BF16), nt,
                                preferred_element_type=F32)
        kd = (k * jnp.exp(last - b)).astype(BF16)
        state_ref[h] = state_t * jnp.exp(last[0:1, :]) + lax.dot_general(v, kd, tn, preferred_element_type=F32)

        o = o * lax.rsqrt(jnp.mean(o * o, axis=-1, keepdims=True) + LN_EPS) * on_ref[:, cols]
        gate = g_ref[:, cols]
        o_ref[:, cols] = (o * (gate * jax.nn.sigmoid(gate))).astype(o_ref.dtype)


def _hgrn(hcat, lb, onorm_g, batch, seq, d_a, chunk=128, heads=4):
    t = hcat.shape[0]
    d_b = lb.shape[0]
    n_heads = d_b // HGRN_HEAD_DIM
    heads = _tile(n_heads, heads)
    bw = heads * HGRN_HEAD_DIM
    chunk = _tile(seq, chunk)
    assert (2 * d_a) % bw == 0 and d_b % bw == 0
    n_chunks = seq // chunk
    base = (2 * d_a) // bw
    per = d_b // bw

    def col(which):
        return pl.BlockSpec((chunk, bw), lambda b, hb, c: (b * n_chunks + c, base + which * per + hb))

    vec = pl.BlockSpec((1, bw), lambda b, hb, c: (0, hb))
    return pl.pallas_call(
        functools.partial(_hgrn_kernel, heads=heads, chunk=chunk),
        grid=(batch, n_heads // heads, n_chunks),
        in_specs=[col(0), col(1), col(2), col(3), vec, vec],
        out_specs=pl.BlockSpec((chunk, bw), lambda b, hb, c: (b * n_chunks + c, hb)),
        out_shape=jax.ShapeDtypeStruct((t, d_b), BF16),
        scratch_shapes=[pltpu.VMEM((heads, HGRN_HEAD_DIM, HGRN_HEAD_DIM), F32)],
        compiler_params=_params(3),
        name="hgrn2",
    )(hcat, hcat, hcat, hcat, lb.reshape(1, d_b), onorm_g.reshape(1, d_b))


def _xattn_kernel(x_ref, wq_ref, k_ref, v_ref, o_ref, *, heads):
    q = jnp.dot(x_ref[...], wq_ref[...], preferred_element_type=F32).astype(BF16)
    d = q.shape[1]
    dh = d // heads
    nt = (((1,), (1,)), ((), ()))
    for h in range(heads):
        cols = slice(h * dh, (h + 1) * dh)
        s = lax.dot_general(q[:, cols], k_ref[:, cols], nt, preferred_element_type=F32) * (dh ** -0.5)
        s = s - jnp.max(s, axis=-1, keepdims=True)
        p = jnp.exp(s)
        p = p / jnp.sum(p, axis=-1, keepdims=True)
        o_ref[:, cols] = jnp.dot(p.astype(BF16), v_ref[:, cols], preferred_element_type=F32).astype(o_ref.dtype)


def _xattn(xb, w_q, kv, batch, seq, tm=512):
    t, d = xb.shape
    m_len = kv.shape[0] // batch
    tm = _tile(seq, tm)
    nt = seq // tm
    return pl.pallas_call(
        functools.partial(_xattn_kernel, heads=XA_HEADS),
        grid=(batch, nt),
        in_specs=[pl.BlockSpec((tm, d), lambda b, i: (b * nt + i, 0)),
                  pl.BlockSpec((d, d), lambda b, i: (0, 0)),
                  pl.BlockSpec((m_len, d), lambda b, i: (b, 0)),
                  pl.BlockSpec((m_len, d), lambda b, i: (b, 1))],
        out_specs=pl.BlockSpec((tm, d), lambda b, i: (b * nt + i, 0)),
        out_shape=jax.ShapeDtypeStruct((t, d), BF16),
        compiler_params=_params(2),
        name="xattn",
    )(xb, w_q, kv, kv)


def _compare_exchange(xs, i, j):
    a, b = xs[i], xs[j]
    xs[i], xs[j] = jnp.maximum(a, b), jnp.minimum(a, b)


def _bitonic_merge_desc(xs):
    n = len(xs)
    j = n // 2
    while j >= 1:
        for i in range(n):
            if i ^ j > i:
                _compare_exchange(xs, i, i ^ j)
        j //= 2


def _bitonic_sort_desc(xs):
    n = len(xs)
    k = 2
    while k <= n:
        j = k // 2
        while j >= 1:
            for i in range(n):
                l = i ^ j
                if l > i:
                    if i & k == 0 or k == n:
                        _compare_exchange(xs, i, l)
                    else:
                        _compare_exchange(xs, l, i)
            j //= 2
        k *= 2


def _top_merge_desc(xs, ys):
    n = len(xs)
    zs = [jnp.maximum(xs[r], ys[n - 1 - r]) for r in range(n)]
    _bitonic_merge_desc(zs)
    return zs


def _top16_rows(s):
    nk = s.shape[0]
    assert nk == 8 * PEER_TOPK
    xs = [s[8 * g:8 * (g + 1), :] for g in range(PEER_TOPK)]
    _bitonic_sort_desc(xs)
    for shift in (4, 2, 1):
        xs = _top_merge_desc(xs, [pltpu.roll(x, shift, axis=0) for x in xs])
    return xs


def _count_true_prefix(pred, ts):
    t8 = pred(ts[7])
    t4 = pred(jnp.where(t8, ts[11], ts[3]))
    t2 = pred(jnp.where(t8, jnp.where(t4, ts[13], ts[9]), jnp.where(t4, ts[5], ts[1])))
    hi = jnp.where(t4, jnp.where(t2, ts[14], ts[12]), jnp.where(t2, ts[10], ts[8]))
    lo = jnp.where(t4, jnp.where(t2, ts[6], ts[4]), jnp.where(t2, ts[2], ts[0]))
    t1 = pred(jnp.where(t8, hi, lo))
    n = (jnp.where(t8, 8.0, 0.0) + jnp.where(t4, 4.0, 0.0)) + (jnp.where(t2, 2.0, 0.0) + jnp.where(t1, 1.0, 0.0))
    return jnp.where(pred(ts[15]), 16.0, n)


def _peer_route_kernel(x_ref, wq_ref, keys_ref, n1_ref, c1_ref, r2_ref, e2_ref, q_ref):
    k = PEER_TOPK
    q_ref[...] = jnp.dot(x_ref[...], wq_ref[...], preferred_element_type=F32).astype(q_ref.dtype)
    half = q_ref.shape[1] // 2
    nk = keys_ref.shape[1]
    nt = (((1,), (1,)), ((), ()))
    width = min(2 * LANES, q_ref.shape[0])

    def lane_tile(lt, carry):
        q = q_ref[pl.ds(pl.multiple_of(lt * width, width), width), :]
        s1 = lax.dot_general(keys_ref[0], q[:, :half], nt, preferred_element_type=F32)
        s2 = lax.dot_general(keys_ref[1], q[:, half:], nt, preferred_element_type=F32)
        a = _top16_rows(s1)
        b = _top16_rows(s2)
        first = [a[0] + b[j] for j in range(k)]
        rest = [a[i] + b[j] for i in range(1, k) for j in range(k) if (i + 1) * (j + 1) <= k]
        neg = jnp.full_like(a[0], -jnp.inf)
        rest = rest + [neg] * (3 * k - len(rest))
        lists = [rest[0:k], rest[k:2 * k], rest[2 * k:3 * k]]
        for l in lists:
            _bitonic_sort_desc(l)
        best = _top_merge_desc(_top_merge_desc(first, lists[0]), _top_merge_desc(lists[1], lists[2]))
        z = jnp.exp(best[0] - best[0])
        for r in range(1, k):
            z = z + jnp.exp(best[r] - best[0])

        rep = lambda slab: jnp.broadcast_to(slab[0:1, :], (nk, width))
        theta = rep(best[k - 1])
        bs = [rep(x) for x in b]
        n1 = _count_true_prefix(lambda br: s1 + br >= theta, bs)
        r2 = _count_true_prefix(lambda br: s2 < br, bs).astype(r2_ref.dtype)
        c1 = jnp.exp(s1 - rep(a[0])) * rep(1.0 / z)
        e2 = jnp.exp(s2 - bs[0]).astype(e2_ref.dtype)
        for w in range(width // LANES):
            tile = lt * (width // LANES) + w
            lanes = slice(w * LANES, (w + 1) * LANES)
            n1_ref[0, tile] = n1[:, lanes]
            r2_ref[0, tile] = r2[:, lanes]
            c1_ref[0, tile] = c1[:, lanes]
            e2_ref[0, tile] = e2[:, lanes]
        return carry

    lax.fori_loop(0, q_ref.shape[0] // width, lane_tile, 0)


def _peer_route(xb, w_q, keys, tm=1024):
    t, d = xb.shape
    nk, half = keys.shape[1], keys.shape[2]
    tm = _tile(t, tm)
    assert tm % LANES == 0
    out = jax.ShapeDtypeStruct((PEER_HEADS, t // LANES, nk, LANES), F32)
    out_b = jax.ShapeDtypeStruct((PEER_HEADS, t // LANES, nk, LANES), BF16)
    ospec = pl.BlockSpec((1, tm // LANES, nk, LANES), lambda i, h: (h, i, 0, 0))
    return pl.pallas_call(
        _peer_route_kernel,
        grid=(t // tm, PEER_HEADS),
        in_specs=[pl.BlockSpec((tm, d), lambda i, h: (i, 0)),
                  pl.BlockSpec((d, 2 * half), lambda i, h: (0, h)),
                  pl.BlockSpec((2, nk, half), lambda i, h: (0, 0, 0))],
        out_specs=[ospec, ospec, ospec, ospec],
        out_shape=[out, out, out_b, out_b],
        scratch_shapes=[pltpu.VMEM((tm, 2 * half), BF16)],
        compiler_params=_params(2),
        name="peer_route",
    )(xb, w_q, keys)


def _expert_gates(gate_ref, n1_ref, c1_ref, r2_ref, e2_ref):
    heads, n_tiles, nk, _ = r2_ref.shape
    zero0 = jnp.zeros((BF16_ROWS, LANES), BF16)
    zero = zero0
    group = 4
    for lt in range(n_tiles):
        lanes = slice(lt * LANES, (lt + 1) * LANES)
        for i in range(n1_ref.shape[2]):
            for j0 in range(0, nk // BF16_ROWS, group):
                gates = [None] * group
                for h in range(heads):
                    nb = jnp.broadcast_to(n1_ref[h, lt, i:i + 1, :], (BF16_ROWS, LANES)).astype(BF16)
                    cb = jnp.broadcast_to(c1_ref[h, lt, i:i + 1, :], (BF16_ROWS, LANES)).astype(BF16)
                    for g in range(group):
                        rows = slice((j0 + g) * BF16_ROWS, (j0 + g + 1) * BF16_ROWS)
                        term = jnp.where(r2_ref[h, lt, rows, :] < nb, e2_ref[h, lt, rows, :] * cb, zero)
                        gates[g] = term if gates[g] is None else gates[g] + term
                zero = jnp.minimum(gates[group - 1], zero0)
                for g in range(group):
                    erows = slice(i * nk + (j0 + g) * BF16_ROWS, i * nk + (j0 + g + 1) * BF16_ROWS)
                    gate_ref[erows, lanes] = gates[g]


def _peer_expert_kernel(xt_ref, u_ref, vt_ref, n1_ref, c1_ref, r2_ref, e2_ref, h_ref, g_ref, b_ref,
                        of_ref, ob_ref, acc_ref, act_ref, hid_ref, *, alpha):
    k = pl.program_id(1)

    @pl.when(k == 0)
    def _():
        acc_ref[...] = jnp.zeros_like(acc_ref)

    _expert_gates(hid_ref, n1_ref, c1_ref, r2_ref, e2_ref)
    act_ref[...] = jnp.dot(u_ref[...], xt_ref[...], preferred_element_type=F32)
    step = LANES
    for r in range(0, hid_ref.shape[0], step):
        hid_ref[r:r + step, :] = jax.nn.gelu(act_ref[r:r + step, :].astype(BF16)) * hid_ref[r:r + step, :]
    acc_ref[...] += jnp.dot(vt_ref[...], hid_ref[...], preferred_element_type=F32)

    @pl.when(k == pl.num_programs(1) - 1)
    def _():
        out = _layer_norm_rows(alpha * h_ref[...] + acc_ref[...].T, g_ref[...], b_ref[...])
        of_ref[...] = out
        ob_ref[...] = out.astype(BF16)


def _peer_experts_ln(xt, u, vt, n1, c1, r2, e2, h, g, b, alpha, tm=512):
    d, t = xt.shape
    ne = u.shape[0]
    heads, _, nk, _ = n1.shape
    tm = _tile(t, tm)
    rows = F32_ROWS
    te = rows * nk
    assert ne % te == 0 and nk % BF16_ROWS == 0 and tm % LANES == 0
    full_spec = pl.BlockSpec((heads, tm // LANES, nk, LANES), lambda i, k: (0, i, 0, 0))
    row_spec = pl.BlockSpec((heads, tm // LANES, rows, LANES), lambda i, k: (0, i, k, 0))
    tok_spec = pl.BlockSpec((tm, d), lambda i, k: (i, 0))
    vec_spec = pl.BlockSpec((1, d), lambda i, k: (0, 0))
    return pl.pallas_call(
        functools.partial(_peer_expert_kernel, alpha=alpha),
        grid=(t // tm, ne // te),
        in_specs=[pl.BlockSpec((d, tm), lambda i, k: (0, i)),
                  pl.BlockSpec((te, d), lambda i, k: (k, 0)),
                  pl.BlockSpec((d, te), lambda i, k: (0, k)),
                  row_spec, row_spec, full_spec, full_spec,
                  pl.BlockSpec((tm, d), lambda i, k: (i, 0), pipeline_mode=pl.Buffered(1)),
                  vec_spec, vec_spec],
        out_specs=[tok_spec, tok_spec],
        out_shape=[jax.ShapeDtypeStruct((t, d), F32), jax.ShapeDtypeStruct((t, d), BF16)],
        scratch_shapes=[pltpu.VMEM((d, tm), F32), pltpu.VMEM((te, tm), F32), pltpu.VMEM((te, tm), BF16)],
        compiler_params=_params(2),
        name="peer_experts",
    )(xt, u, vt, n1, c1, r2, e2, h, g.reshape(1, d), b.reshape(1, d))


def kernel(x, mem, ev_w_in, ev_sgu_ln_g, ev_sgu_ln_b, ev_w_s, ev_b_s, ev_lb_logits, ev_onorm_g, ev_w_out, od_w_in, od_conv_w, od_w_out, mix_ln_g, mix_ln_b, xa_w_q, xa_w_kv, xa_w_o, xa_ln_g, xa_ln_b, peer_w_q, peer_keys, peer_u, peer_v, ffn_ln_g, ffn_ln_b):
    batch, seq, d = x.shape
    depth = mix_ln_g.shape[0]
    alpha = (2.0 * depth) ** 0.25
    t = batch * seq
    d_a = ev_sgu_ln_g.shape[1]

    p = jax.nn.softmax(ev_lb_logits.astype(F32), axis=0)
    lower_bounds = jnp.clip(jnp.cumsum(p, axis=0) - p[0], 0.0, 1.0)

    h = x.reshape(t, d)
    hb = h.astype(BF16)
    memb = mem.reshape(-1, d).astype(BF16)
    for layer in range(depth):
        j = layer // 2
        if layer % 2 == 0:
            hcat = _matmul(hb, ev_w_in[j].astype(BF16), F32)
            a_out = _sgu(hcat, ev_sgu_ln_g[j], ev_sgu_ln_b[j], ev_w_s[j], ev_b_s[j])
            b_out = _hgrn(hcat, lower_bounds[j], ev_onorm_g[j], batch, seq, d_a)
            w_out = ev_w_out[j].astype(BF16)
            us, ws = [a_out, b_out], [w_out[:d_a], w_out[d_a:]]
        else:
            us = [_shortconv(hb, od_w_in[j].astype(BF16), od_conv_w[j], seq)]
            ws = [od_w_out[j].astype(BF16)]
        h, hb = _proj_res_ln(us, ws, h, mix_ln_g[layer], mix_ln_b[layer], alpha)

        kv = _matmul(memb, xa_w_kv[layer].astype(BF16), BF16)
        att = _xattn(hb, xa_w_q[layer].astype(BF16), kv, batch, seq)
        h, hb = _proj_res_ln([att], [xa_w_o[layer].astype(BF16)], h, xa_ln_g[layer], xa_ln_b[layer], alpha)

        n1, c1, r2, e2 = _peer_route(hb, peer_w_q[layer].astype(BF16), peer_keys[layer].astype(BF16))
        h, hb = _peer_experts_ln(hb.T, peer_u[layer].astype(BF16), peer_v[layer].T.astype(BF16), n1, c1, r2, e2,
                                 h, ffn_ln_g[layer], ffn_ln_b[layer], alpha)
    return h.reshape(batch, seq, d)
```

```python
import functools

import jax
import jax.numpy as jnp
from jax import lax
from jax.experimental import pallas as pl
from jax.experimental.pallas import tpu as pltpu

LN_EPS = 1e-5
F_FLOOR = 1e-30
HGRN_HEAD_DIM = 128
XA_HEADS = 4
PEER_HEADS = 8
PEER_TOPK = 16
V7X_VMEM_LIMIT_BYTES = 56 * 1024 * 1024
LANES = 128
F32_ROWS = 8
BF16_ROWS = 16
F32 = jnp.float32
BF16 = jnp.bfloat16


def _params(n_axes, flags=None):
    return pltpu.CompilerParams(dimension_semantics=("arbitrary",) * n_axes,
                                vmem_limit_bytes=V7X_VMEM_LIMIT_BYTES, flags=flags)


def _tile(n, want):
    t = min(n, want)
    while n % t:
        t -= LANES
    assert t > 0, (n, want)
    return t


def _layer_norm_rows(y, g, b):
    mu = jnp.mean(y, axis=-1, keepdims=True)
    yc = y - mu
    var = jnp.mean(yc * yc, axis=-1, keepdims=True)
    return yc * lax.rsqrt(var + LN_EPS) * g + b


def _mm_kernel(x_ref, w_ref, o_ref):
    o_ref[...] = jnp.dot(x_ref[...], w_ref[...], preferred_element_type=F32).astype(o_ref.dtype)


def _matmul(x, w, out_dtype, tm=1024, tn=1024):
    m, k = x.shape
    n = w.shape[1]
    tm, tn = _tile(m, tm), _tile(n, tn)
    return pl.pallas_call(
        _mm_kernel,
        grid=(n // tn, m // tm),
        in_specs=[pl.BlockSpec((tm, k), lambda j, i: (i, 0)),
                  pl.BlockSpec((k, tn), lambda j, i: (0, j))],
        out_specs=pl.BlockSpec((tm, tn), lambda j, i: (i, j)),
        out_shape=jax.ShapeDtypeStruct((m, n), out_dtype),
        compiler_params=_params(2),
        name="matmul",
    )(x, w)


def _proj_ln_kernel(*refs, n_in, alpha):
    us, ws = refs[:n_in], refs[n_in:2 * n_in]
    h_ref, g_ref, b_ref, of_ref, ob_ref = refs[2 * n_in:]
    acc = jnp.dot(us[0][...], ws[0][...], preferred_element_type=F32)
    for u_ref, w_ref in zip(us[1:], ws[1:]):
        acc = acc + jnp.dot(u_ref[...], w_ref[...], preferred_element_type=F32)
    out = _layer_norm_rows(alpha * h_ref[...] + acc, g_ref[...], b_ref[...])
    of_ref[...] = out
    ob_ref[...] = out.astype(BF16)


def _proj_res_ln(us, ws, h, g, b, alpha, tm=256):
    t, d = h.shape
    tm = _tile(t, tm)
    n_in = len(us)
    in_specs = [pl.BlockSpec((tm, u.shape[1]), lambda i: (i, 0)) for u in us]
    in_specs += [pl.BlockSpec(w.shape, lambda i: (0, 0)) for w in ws]
    in_specs += [pl.BlockSpec((tm, d), lambda i: (i, 0)),
                 pl.BlockSpec((1, d), lambda i: (0, 0)),
                 pl.BlockSpec((1, d), lambda i: (0, 0))]
    return pl.pallas_call(
        functools.partial(_proj_ln_kernel, n_in=n_in, alpha=alpha),
        grid=(t // tm,),
        in_specs=in_specs,
        out_specs=[pl.BlockSpec((tm, d), lambda i: (i, 0)), pl.BlockSpec((tm, d), lambda i: (i, 0))],
        out_shape=[jax.ShapeDtypeStruct((t, d), F32), jax.ShapeDtypeStruct((t, d), BF16)],
        compiler_params=_params(1),
        name="proj_res_ln",
    )(*us, *ws, h, g.reshape(1, d), b.reshape(1, d))


def _shortconv_kernel(x_ref, wb_ref, wc_ref, wx_ref, cw_ref, o_ref, carry_ref, *, tiles_per_seq):
    i = pl.program_id(1)

    @pl.when(i % tiles_per_seq == 0)
    def _():
        carry_ref[...] = jnp.zeros_like(carry_ref)

    x = x_ref[...]
    gate_b = jnp.dot(x, wb_ref[...], preferred_element_type=F32)
    gate_c = jnp.dot(x, wc_ref[...], preferred_element_type=F32)
    xin = jnp.dot(x, wx_ref[...], preferred_element_type=F32)
    z = gate_c * xin
    tm = z.shape[0]
    prev = carry_ref[...]
    p1, p2 = prev[7:8, :], prev[6:7, :]
    row = lax.broadcasted_iota(jnp.int32, z.shape, 0)
    z1 = jnp.where(row == 0, p1, pltpu.roll(z, 1, axis=0))
    z2 = jnp.where(row == 0, p2, jnp.where(row == 1, p1, pltpu.roll(z, 2, axis=0)))
    cw = cw_ref[...]
    y = cw[0:1, :] * z2 + cw[1:2, :] * z1 + cw[2:3, :] * z
    o_ref[...] = (gate_b * y).astype(o_ref.dtype)
    carry_ref[...] = z[tm - 8:tm, :]


def _shortconv(xb, w_in, conv_w, seq, tm=512, tc=512):
    t, d = xb.shape
    tm, tc = _tile(seq, tm), _tile(d, tc)
    nc = d // tc
    cw = jnp.zeros((8, d), F32).at[:conv_w.shape[0]].set(conv_w)
    wspec = lambda off: pl.BlockSpec((d, tc), lambda c, i: (0, off + c))
    return pl.pallas_call(
        functools.partial(_shortconv_kernel, tiles_per_seq=seq // tm),
        grid=(nc, t // tm),
        in_specs=[pl.BlockSpec((tm, d), lambda c, i: (i, 0)),
                  wspec(0), wspec(nc), wspec(2 * nc),
                  pl.BlockSpec((8, tc), lambda c, i: (0, c))],
        out_specs=pl.BlockSpec((tm, tc), lambda c, i: (i, c)),
        out_shape=jax.ShapeDtypeStruct((t, d), BF16),
        scratch_shapes=[pltpu.VMEM((8, tc), F32)],
        compiler_params=_params(2),
        name="shortconv",
    )(xb, w_in, w_in, w_in, cw)


def _sgu_kernel(au_ref, av_ref, g_ref, b_ref, ws_ref, bias_ref, o_ref, *, groups, chunk):
    u = jax.nn.gelu(au_ref[...])
    v = _layer_norm_rows(jax.nn.gelu(av_ref[...]), g_ref[...], b_ref[...]).astype(BF16)
    tm, da = u.shape
    gd = da // groups
    r = lax.broadcasted_iota(jnp.int32, (chunk, chunk), 0)
    c = lax.broadcasted_iota(jnp.int32, (chunk, chunk), 1)
    causal = r >= c
    for g in range(groups):
        w = jnp.where(causal, ws_ref[g], 0.0).astype(BF16)
        cols = slice(g * gd, (g + 1) * gd)
        for n in range(tm // chunk):
            rows = slice(n * chunk, (n + 1) * chunk)
            mixed = jnp.dot(w, v[rows, cols], preferred_element_type=F32) + bias_ref[:, cols]
            o_ref[rows, cols] = (u[rows, cols] * mixed).astype(o_ref.dtype)


def _sgu(hcat, ln_g, ln_b, w_s, b_s, tm=256):
    t = hcat.shape[0]
    groups, chunk, _ = w_s.shape
    da = ln_g.shape[0]
    gd = da // groups
    tm = _tile(t, tm)
    assert tm % chunk == 0
    bias = jnp.repeat(b_s.T, gd, axis=1)
    return pl.pallas_call(
        functools.partial(_sgu_kernel, groups=groups, chunk=chunk),
        grid=(t // tm,),
        in_specs=[pl.BlockSpec((tm, da), lambda i: (i, 0)),
                  pl.BlockSpec((tm, da), lambda i: (i, 1)),
                  pl.BlockSpec((1, da), lambda i: (0, 0)),
                  pl.BlockSpec((1, da), lambda i: (0, 0)),
                  pl.BlockSpec((groups, chunk, chunk), lambda i: (0, 0, 0)),
                  pl.BlockSpec((chunk, da), lambda i: (0, 0))],
        out_specs=pl.BlockSpec((tm, da), lambda i: (i, 0)),
        out_shape=jax.ShapeDtypeStruct((t, da), BF16),
        compiler_params=_params(1),
        name="sgu",
    )(hcat, hcat, ln_g.reshape(1, da), ln_b.reshape(1, da), w_s, bias)


def _hgrn_kernel(q_ref, f_ref, i_ref, g_ref, lb_ref, on_ref, o_ref, state_ref, *, heads, chunk):
    dh = HGRN_HEAD_DIM

    @pl.when(pl.program_id(2) == 0)
    def _():
        state_ref[...] = jnp.zeros_like(state_ref)

    row = lax.broadcasted_iota(jnp.int32, (chunk, dh), 0)
    tt = lax.broadcasted_iota(jnp.int32, (chunk, chunk), 0)
    ss = lax.broadcasted_iota(jnp.int32, (chunk, chunk), 1)
    level = jnp.where(tt > ss, 31 - lax.clz(tt ^ ss), jnp.where(tt == ss, -1, -2))
    nt = (((1,), (1,)), ((), ()))
    tn = (((0,), (0,)), ((), ()))

    for h in range(heads):
        cols = slice(h * dh, (h + 1) * dh)
        z = f_ref[:, cols]
        lb = lb_ref[:, cols]
        sig = jax.nn.sigmoid(z)
        f = lb + (1.0 - lb) * sig
        logf = jnp.log(jnp.maximum(f, F_FLOOR))
        k = (1.0 - lb) * jax.nn.sigmoid(-z)
        q = q_ref[:, cols]
        q = q * jax.nn.sigmoid(q)
        v = i_ref[:, cols].astype(BF16)

        b = logf
        sh = 1
        while sh < chunk:
            b = b + jnp.where(row >= sh, pltpu.roll(b, sh, axis=0), 0.0)
            sh *= 2

        scores = jnp.where(level == -1, lax.dot_general(q.astype(BF16), k.astype(BF16), nt,
                                                        preferred_element_type=F32), 0.0)
        before = jnp.where(row >= 1, pltpu.roll(b, 1, axis=0), 0.0)
        last = b
        m, lg = 1, 0
        while m < chunk:
            qd = (q * jnp.exp(b - before)).astype(BF16)
            kd = (k * jnp.exp(last - b)).astype(BF16)
            lvl = lax.dot_general(qd, kd, nt, preferred_element_type=F32)
            scores = jnp.where(level == lg, lvl, scores)
            bit = (row & m) != 0
            before = jnp.where(bit, pltpu.roll(before, m, axis=0), before)
            last = jnp.where(bit, last, pltpu.roll(last, chunk - m, axis=0))
            m, lg = 2 * m, lg + 1
        state_t = state_ref[h]
        o = jnp.dot(scores.astype(BF16), v, preferred_element_type=F32)
        o = o + lax.dot_general((q * jnp.exp(b)).astype(BF16), state_t.astype(BF16), nt,
                                preferred_element_type=F32)
        kd = (k * jnp.exp(last - b)).astype(BF16)
        state_ref[h] = state_t * jnp.exp(last[0:1, :]) + lax.dot_general(v, kd, tn, preferred_element_type=F32)

        o = o * lax.rsqrt(jnp.mean(o * o, axis=-1, keepdims=True) + LN_EPS) * on_ref[:, cols]
        gate = g_ref[:, cols]
        o_ref[:, cols] = (o * (gate * jax.nn.sigmoid(gate))).astype(o_ref.dtype)


def _hgrn(hcat, lb, onorm_g, batch, seq, d_a, chunk=128, heads=4):
    t = hcat.shape[0]
    d_b = lb.shape[0]
    n_heads = d_b // HGRN_HEAD_DIM
    heads = _tile(n_heads, heads)
    bw = heads * HGRN_HEAD_DIM
    chunk = _tile(seq, chunk)
    assert (2 * d_a) % bw == 0 and d_b % bw == 0
    n_chunks = seq // chunk
    base = (2 * d_a) // bw
    per = d_b // bw

    def col(which):
        return pl.BlockSpec((chunk, bw), lambda b, hb, c: (b * n_chunks + c, base + which * per + hb))

    vec = pl.BlockSpec((1, bw), lambda b, hb, c: (0, hb))
    return pl.pallas_call(
        functools.partial(_hgrn_kernel, heads=heads, chunk=chunk),
        grid=(batch, n_heads // heads, n_chunks),
        in_specs=[col(0), col(1), col(2), col(3), vec, vec],
        out_specs=pl.BlockSpec((chunk, bw), lambda b, hb, c: (b * n_chunks + c, hb)),
        out_shape=jax.ShapeDtypeStruct((t, d_b), BF16),
        scratch_shapes=[pltpu.VMEM((heads, HGRN_HEAD_DIM, HGRN_HEAD_DIM), F32)],
        compiler_params=_params(3),
        name="hgrn2",
    )(hcat, hcat, hcat, hcat, lb.reshape(1, d_b), onorm_g.reshape(1, d_b))


def _xattn_kernel(x_ref, wq_ref, k_ref, v_ref, o_ref, *, heads):
    q = jnp.dot(x_ref[...], wq_ref[...], preferred_element_type=F32).astype(BF16)
    d = q.shape[1]
    dh = d // heads
    nt = (((1,), (1,)), ((), ()))
    for h in range(heads):
        cols = slice(h * dh, (h + 1) * dh)
        s = lax.dot_general(q[:, cols], k_ref[:, cols], nt, preferred_element_type=F32) * (dh ** -0.5)
        s = s - jnp.max(s, axis=-1, keepdims=True)
        p = jnp.exp(s)
        p = p / jnp.sum(p, axis=-1, keepdims=True)
        o_ref[:, cols] = jnp.dot(p.astype(BF16), v_ref[:, cols], preferred_element_type=F32).astype(o_ref.dtype)


def _xattn(xb, w_q, kv, batch, seq, tm=512):
    t, d = xb.shape
    m_len = kv.shape[0] // batch
    tm = _tile(seq, tm)
    nt = seq // tm
    return pl.pallas_call(
        functools.partial(_xattn_kernel, heads=XA_HEADS),
        grid=(batch, nt),
        in_specs=[pl.BlockSpec((tm, d), lambda b, i: (b * nt + i, 0)),
                  pl.BlockSpec((d, d), lambda b, i: (0, 0)),
                  pl.BlockSpec((m_len, d), lambda b, i: (b, 0)),
                  pl.BlockSpec((m_len, d), lambda b, i: (b, 1))],
        out_specs=pl.BlockSpec((tm, d), lambda b, i: (b * nt + i, 0)),
        out_shape=jax.ShapeDtypeStruct((t, d), BF16),
        compiler_params=_params(2),
        name="xattn",
    )(xb, w_q, kv, kv)


def _compare_exchange(xs, i, j):
    a, b = xs[i], xs[j]
    xs[i], xs[j] = jnp.maximum(a, b), jnp.minimum(a, b)


def _bitonic_merge_desc(xs):
    n = len(xs)
    j = n // 2
    while j >= 1:
        for i in range(n):
            if i ^ j > i:
                _compare_exchange(xs, i, i ^ j)
        j //= 2


def _bitonic_sort_desc(xs):
    n = len(xs)
    k = 2
    while k <= n:
        j = k // 2
        while j >= 1:
            for i in range(n):
                l = i ^ j
                if l > i:
                    if i & k == 0 or k == n:
                        _compare_exchange(xs, i, l)
                    else:
                        _compare_exchange(xs, l, i)
            j //= 2
        k *= 2


def _top_merge_desc(xs, ys):
    n = len(xs)
    zs = [jnp.maximum(xs[r], ys[n - 1 - r]) for r in range(n)]
    _bitonic_merge_desc(zs)
    return zs


def _top16_rows(s):
    nk = s.shape[0]
    assert nk == 8 * PEER_TOPK
    xs = [s[8 * g:8 * (g + 1), :] for g in range(PEER_TOPK)]
    _bitonic_sort_desc(xs)
    for shift in (4, 2, 1):
        xs = _top_merge_desc(xs, [pltpu.roll(x, shift, axis=0) for x in xs])
    return xs


def _count_true_prefix(pred, ts):
    t8 = pred(ts[7])
    t4 = pred(jnp.where(t8, ts[11], ts[3]))
    t2 = pred(jnp.where(t8, jnp.where(t4, ts[13], ts[9]), jnp.where(t4, ts[5], ts[1])))
    hi = jnp.where(t4, jnp.where(t2, ts[14], ts[12]), jnp.where(t2, ts[10], ts[8]))
    lo = jnp.where(t4, jnp.where(t2, ts[6], ts[4]), jnp.where(t2, ts[2], ts[0]))
    t1 = pred(jnp.where(t8, hi, lo))
    n = (jnp.where(t8, 8.0, 0.0) + jnp.where(t4, 4.0, 0.0)) + (jnp.where(t2, 2.0, 0.0) + jnp.where(t1, 1.0, 0.0))
    return jnp.where(pred(ts[15]), 16.0, n)


def _peer_route_kernel(x_ref, wq_ref, keys_ref, n1_ref, c1_ref, r2_ref, e2_ref, q_ref):
    k = PEER_TOPK
    q_ref[...] = jnp.dot(x_ref[...], wq_ref[...], preferred_element_type=F32).astype(q_ref.dtype)
    half = q_ref.shape[1] // 2
    nk = keys_ref.shape[1]
    nt = (((1,), (1,)), ((), ()))
    width = min(2 * LANES, q_ref.shape[0])

    def lane_tile(lt, carry):
        q = q_ref[pl.ds(pl.multiple_of(lt * width, width), width), :]
        s1 = lax.dot_general(keys_ref[0], q[:, :half], nt, preferred_element_type=F32)
        s2 = lax.dot_general(keys_ref[1], q[:, half:], nt, preferred_element_type=F32)
        a = _top16_rows(s1)
        b = _top16_rows(s2)
        first = [a[0] + b[j] for j in range(k)]
        rest = [a[i] + b[j] for i in range(1, k) for j in range(k) if (i + 1) * (j + 1) <= k]
        neg = jnp.full_like(a[0], -jnp.inf)
        rest = rest + [neg] * (3 * k - len(rest))
        lists = [rest[0:k], rest[k:2 * k], rest[2 * k:3 * k]]
        for l in lists:
            _bitonic_sort_desc(l)
        best = _top_merge_desc(_top_merge_desc(first, lists[0]), _top_merge_desc(lists[1], lists[2]))
        z = jnp.exp(best[0] - best[0])
        for r in range(1, k):
            z = z + jnp.exp(best[r] - best[0])

        rep = lambda slab: jnp.broadcast_to(slab[0:1, :], (nk, width))
        theta = rep(best[k - 1])
        bs = [rep(x) for x in b]
        n1 = _count_true_prefix(lambda br: s1 + br >= theta, bs)
        r2 = _count_true_prefix(lambda br: s2 < br, bs).astype(r2_ref.dtype)
        c1 = jnp.exp(s1 - rep(a[0])) * rep(1.0 / z)
        e2 = jnp.exp(s2 - bs[0]).astype(e2_ref.dtype)
        for w in range(width // LANES):
            tile = lt * (width // LANES) + w
            lanes = slice(w * LANES, (w + 1) * LANES)
            n1_ref[0, tile] = n1[:, lanes]
            r2_ref[0, tile] = r2[:, lanes]
            c1_ref[0, tile] = c1[:, lanes]
            e2_ref[0, tile] = e2[:, lanes]
        return carry

    lax.fori_loop(0, q_ref.shape[0] // width, lane_tile, 0)


def _peer_route(xb, w_q, keys, tm=1024):
    t, d = xb.shape
    nk, half = keys.shape[1], keys.shape[2]
    tm = _tile(t, tm)
    assert tm % LANES == 0
    out = jax.ShapeDtypeStruct((PEER_HEADS, t // LANES, nk, LANES), F32)
    out_b = jax.ShapeDtypeStruct((PEER_HEADS, t // LANES, nk, LANES), BF16)
    ospec = pl.BlockSpec((1, tm // LANES, nk, LANES), lambda i, h: (h, i, 0, 0))
    return pl.pallas_call(
        _peer_route_kernel,
        grid=(t // tm, PEER_HEADS),
        in_specs=[pl.BlockSpec((tm, d), lambda i, h: (i, 0)),
                  pl.BlockSpec((d, 2 * half), lambda i, h: (0, h)),
                  pl.BlockSpec((2, nk, half), lambda i, h: (0, 0, 0))],
        out_specs=[ospec, ospec, ospec, ospec],
        out_shape=[out, out, out_b, out_b],
        scratch_shapes=[pltpu.VMEM((tm, 2 * half), BF16)],
        compiler_params=_params(2),
        name="peer_route",
    )(xb, w_q, keys)


def _expert_gates(gate_ref, n1_ref, c1_ref, r2_ref, e2_ref):
    heads, n_tiles, nk, _ = r2_ref.shape
    zero0 = jnp.zeros((BF16_ROWS, LANES), BF16)
    zero = zero0
    for lt in range(n_tiles):
        lanes = slice(lt * LANES, (lt + 1) * LANES)
        for i in range(n1_ref.shape[2]):
            for jt in range(nk // BF16_ROWS):
                jrows = slice(jt * BF16_ROWS, (jt + 1) * BF16_ROWS)
                gate = None
                for h in range(heads):
                    nb = jnp.broadcast_to(n1_ref[h, lt, i:i + 1, :], (BF16_ROWS, LANES)).astype(BF16)
                    cb = jnp.broadcast_to(c1_ref[h, lt, i:i + 1, :], (BF16_ROWS, LANES)).astype(BF16)
                    term = jnp.where(r2_ref[h, lt, jrows, :] < nb, e2_ref[h, lt, jrows, :] * cb, zero)
                    gate = term if gate is None else gate + term
                zero = jnp.minimum(gate, zero0)
                gate_ref[i * nk + jt * BF16_ROWS:i * nk + (jt + 1) * BF16_ROWS, lanes] = gate


def _peer_expert_kernel(xt_ref, u_ref, vt_ref, n1_ref, c1_ref, r2_ref, e2_ref, h_ref, g_ref, b_ref,
                        of_ref, ob_ref, acc_ref, act_ref, hid_ref, *, alpha):
    k = pl.program_id(1)

    @pl.when(k == 0)
    def _():
        acc_ref[...] = jnp.zeros_like(acc_ref)

    _expert_gates(hid_ref, n1_ref, c1_ref, r2_ref, e2_ref)
    act_ref[...] = jnp.dot(u_ref[...], xt_ref[...], preferred_element_type=F32).astype(act_ref.dtype)
    step = LANES
    for r in range(0, hid_ref.shape[0], step):
        hid_ref[r:r + step, :] = jax.nn.gelu(act_ref[r:r + step, :]) * hid_ref[r:r + step, :]
    acc_ref[...] += jnp.dot(vt_ref[...], hid_ref[...], preferred_element_type=F32)

    @pl.when(k == pl.num_programs(1) - 1)
    def _():
        out = _layer_norm_rows(alpha * h_ref[...] + acc_ref[...].T, g_ref[...], b_ref[...])
        of_ref[...] = out
        ob_ref[...] = out.astype(BF16)


def _peer_experts_ln(xt, u, vt, n1, c1, r2, e2, h, g, b, alpha, tm=512):
    d, t = xt.shape
    ne = u.shape[0]
    heads, _, nk, _ = n1.shape
    tm = _tile(t, tm)
    rows = F32_ROWS
    te = rows * nk
    assert ne % te == 0 and nk % BF16_ROWS == 0 and tm % LANES == 0
    full_spec = pl.BlockSpec((heads, tm // LANES, nk, LANES), lambda i, k: (0, i, 0, 0))
    row_spec = pl.BlockSpec((heads, tm // LANES, rows, LANES), lambda i, k: (0, i, k, 0))
    tok_spec = pl.BlockSpec((tm, d), lambda i, k: (i, 0))
    vec_spec = pl.BlockSpec((1, d), lambda i, k: (0, 0))
    return pl.pallas_call(
        functools.partial(_peer_expert_kernel, alpha=alpha),
        grid=(t // tm, ne // te),
        in_specs=[pl.BlockSpec((d, tm), lambda i, k: (0, i)),
                  pl.BlockSpec((te, d), lambda i, k: (k, 0)),
                  pl.BlockSpec((d, te), lambda i, k: (0, k)),
                  row_spec, row_spec, full_spec, full_spec,
                  pl.BlockSpec((tm, d), lambda i, k: (i, 0), pipeline_mode=pl.Buffered(1)),
                  vec_spec, vec_spec],
        out_specs=[tok_spec, tok_spec],
        out_shape=[jax.ShapeDtypeStruct((t, d), F32), jax.ShapeDtypeStruct((t, d), BF16)],
        scratch_shapes=[pltpu.VMEM((d, tm), F32), pltpu.VMEM((te, tm), BF16), pltpu.VMEM((te, tm), BF16)],
        compiler_params=_params(2),
        name="peer_experts",
    )(xt, u, vt, n1, c1, r2, e2, h, g.reshape(1, d), b.reshape(1, d))


def kernel(x, mem, ev_w_in, ev_sgu_ln_g, ev_sgu_ln_b, ev_w_s, ev_b_s, ev_lb_logits, ev_onorm_g, ev_w_out, od_w_in, od_conv_w, od_w_out, mix_ln_g, mix_ln_b, xa_w_q, xa_w_kv, xa_w_o, xa_ln_g, xa_ln_b, peer_w_q, peer_keys, peer_u, peer_v, ffn_ln_g, ffn_ln_b):
    batch, seq, d = x.shape
    depth = mix_ln_g.shape[0]
    alpha = (2.0 * depth) ** 0.25
    t = batch * seq
    d_a = ev_sgu_ln_g.shape[1]

    p = jax.nn.softmax(ev_lb_logits.astype(F32), axis=0)
    lower_bounds = jnp.clip(jnp.cumsum(p, axis=0) - p[0], 0.0, 1.0)

    h = x.reshape(t, d)
    hb = h.astype(BF16)
    memb = mem.reshape(-1, d).astype(BF16)
    for layer in range(depth):
        j = layer // 2
        if layer % 2 == 0:
            hcat = _matmul(hb, ev_w_in[j].astype(BF16), F32)
            a_out = _sgu(hcat, ev_sgu_ln_g[j], ev_sgu_ln_b[j], ev_w_s[j], ev_b_s[j])
            b_out = _hgrn(hcat, lower_bounds[j], ev_onorm_g[j], batch, seq, d_a)
            w_out = ev_w_out[j].astype(BF16)
            us, ws = [a_out, b_out], [w_out[:d_a], w_out[d_a:]]
        else:
            us = [_shortconv(hb, od_w_in[j].astype(BF16), od_conv_w[j], seq)]
            ws = [od_w_out[j].astype(BF16)]
        h, hb = _proj_res_ln(us, ws, h, mix_ln_g[layer], mix_ln_b[layer], alpha)

        kv = _matmul(memb, xa_w_kv[layer].astype(BF16), BF16)
        att = _xattn(hb, xa_w_q[layer].astype(BF16), kv, batch, seq)
        h, hb = _proj_res_ln([att], [xa_w_o[layer].astype(BF16)], h, xa_ln_g[layer], xa_ln_b[layer], alpha)

        n1, c1, r2, e2 = _peer_route(hb, peer_w_q[layer].astype(BF16), peer_keys[layer].astype(BF16))
        h, hb = _peer_experts_ln(hb.T, peer_u[layer].astype(BF16), peer_v[layer].T.astype(BF16), n1, c1, r2, e2,
                                 h, ffn_ln_g[layer], ffn_ln_b[layer], alpha)
    return h.reshape(batch, seq, d)
```

```python
import functools

import jax
import jax.numpy as jnp
from jax import lax
from jax.experimental import pallas as pl
from jax.experimental.pallas import tpu as pltpu

LN_EPS = 1e-5
F_FLOOR = 1e-30
HGRN_HEAD_DIM = 128
XA_HEADS = 4
PEER_HEADS = 8
PEER_TOPK = 16
V7X_VMEM_LIMIT_BYTES = 56 * 1024 * 1024
LANES = 128
F32_ROWS = 8
BF16_ROWS = 16
F32 = jnp.float32
BF16 = jnp.bfloat16


def _params(n_axes, flags=None):
    return pltpu.CompilerParams(dimension_semantics=("arbitrary",) * n_axes,
                                vmem_limit_bytes=V7X_VMEM_LIMIT_BYTES, flags=flags)


def _tile(n, want):
    t = min(n, want)
    while n % t:
        t -= LANES
    assert t > 0, (n, want)
    return t


def _layer_norm_rows(y, g, b):
    mu = jnp.mean(y, axis=-1, keepdims=True)
    yc = y - mu
    var = jnp.mean(yc * yc, axis=-1, keepdims=True)
    return yc * lax.rsqrt(var + LN_EPS) * g + b


def _mm_kernel(x_ref, w_ref, o_ref):
    o_ref[...] = jnp.dot(x_ref[...], w_ref[...], preferred_element_type=F32).astype(o_ref.dtype)


def _matmul(x, w, out_dtype, tm=1024, tn=1024):
    m, k = x.shape
    n = w.shape[1]
    tm, tn = _tile(m, tm), _tile(n, tn)
    return pl.pallas_call(
        _mm_kernel,
        grid=(n // tn, m // tm),
        in_specs=[pl.BlockSpec((tm, k), lambda j, i: (i, 0)),
                  pl.BlockSpec((k, tn), lambda j, i: (0, j))],
        out_specs=pl.BlockSpec((tm, tn), lambda j, i: (i, j)),
        out_shape=jax.ShapeDtypeStruct((m, n), out_dtype),
        compiler_params=_params(2),
        name="matmul",
    )(x, w)


def _proj_ln_kernel(*refs, n_in, alpha):
    us, ws = refs[:n_in], refs[n_in:2 * n_in]
    h_ref, g_ref, b_ref, of_ref, ob_ref = refs[2 * n_in:]
    acc = jnp.dot(us[0][...], ws[0][...], preferred_element_type=F32)
    for u_ref, w_ref in zip(us[1:], ws[1:]):
        acc = acc + jnp.dot(u_ref[...], w_ref[...], preferred_element_type=F32)
    out = _layer_norm_rows(alpha * h_ref[...] + acc, g_ref[...], b_ref[...])
    of_ref[...] = out
    ob_ref[...] = out.astype(BF16)


def _proj_res_ln(us, ws, h, g, b, alpha, tm=256):
    t, d = h.shape
    tm = _tile(t, tm)
    n_in = len(us)
    in_specs = [pl.BlockSpec((tm, u.shape[1]), lambda i: (i, 0)) for u in us]
    in_specs += [pl.BlockSpec(w.shape, lambda i: (0, 0)) for w in ws]
    in_specs += [pl.BlockSpec((tm, d), lambda i: (i, 0)),
                 pl.BlockSpec((1, d), lambda i: (0, 0)),
                 pl.BlockSpec((1, d), lambda i: (0, 0))]
    return pl.pallas_call(
        functools.partial(_proj_ln_kernel, n_in=n_in, alpha=alpha),
        grid=(t // tm,),
        in_specs=in_specs,
        out_specs=[pl.BlockSpec((tm, d), lambda i: (i, 0)), pl.BlockSpec((tm, d), lambda i: (i, 0))],
        out_shape=[jax.ShapeDtypeStruct((t, d), F32), jax.ShapeDtypeStruct((t, d), BF16)],
        compiler_params=_params(1),
        name="proj_res_ln",
    )(*us, *ws, h, g.reshape(1, d), b.reshape(1, d))


def _shortconv_kernel(x_ref, wb_ref, wc_ref, wx_ref, cw_ref, o_ref, carry_ref, *, tiles_per_seq):
    i = pl.program_id(1)

    @pl.when(i % tiles_per_seq == 0)
    def _():
        carry_ref[...] = jnp.zeros_like(carry_ref)

    x = x_ref[...]
    gate_b = jnp.dot(x, wb_ref[...], preferred_element_type=F32)
    gate_c = jnp.dot(x, wc_ref[...], preferred_element_type=F32)
    xin = jnp.dot(x, wx_ref[...], preferred_element_type=F32)
    z = gate_c * xin
    tm = z.shape[0]
    prev = carry_ref[...]
    p1, p2 = prev[7:8, :], prev[6:7, :]
    row = lax.broadcasted_iota(jnp.int32, z.shape, 0)
    z1 = jnp.where(row == 0, p1, pltpu.roll(z, 1, axis=0))
    z2 = jnp.where(row == 0, p2, jnp.where(row == 1, p1, pltpu.roll(z, 2, axis=0)))
    cw = cw_ref[...]
    y = cw[0:1, :] * z2 + cw[1:2, :] * z1 + cw[2:3, :] * z
    o_ref[...] = (gate_b * y).astype(o_ref.dtype)
    carry_ref[...] = z[tm - 8:tm, :]


def _shortconv(xb, w_in, conv_w, seq, tm=512, tc=512):
    t, d = xb.shape
    tm, tc = _tile(seq, tm), _tile(d, tc)
    nc = d // tc
    cw = jnp.zeros((8, d), F32).at[:conv_w.shape[0]].set(conv_w)
    wspec = lambda off: pl.BlockSpec((d, tc), lambda c, i: (0, off + c))
    return pl.pallas_call(
        functools.partial(_shortconv_kernel, tiles_per_seq=seq // tm),
        grid=(nc, t // tm),
        in_specs=[pl.BlockSpec((tm, d), lambda c, i: (i, 0)),
                  wspec(0), wspec(nc), wspec(2 * nc),
                  pl.BlockSpec((8, tc), lambda c, i: (0, c))],
        out_specs=pl.BlockSpec((tm, tc), lambda c, i: (i, c)),
        out_shape=jax.ShapeDtypeStruct((t, d), BF16),
        scratch_shapes=[pltpu.VMEM((8, tc), F32)],
        compiler_params=_params(2),
        name="shortconv",
    )(xb, w_in, w_in, w_in, cw)


def _sgu_kernel(au_ref, av_ref, g_ref, b_ref, ws_ref, bias_ref, o_ref, *, groups, chunk):
    u = jax.nn.gelu(au_ref[...])
    v = _layer_norm_rows(jax.nn.gelu(av_ref[...]), g_ref[...], b_ref[...]).astype(BF16)
    tm, da = u.shape
    gd = da // groups
    r = lax.broadcasted_iota(jnp.int32, (chunk, chunk), 0)
    c = lax.broadcasted_iota(jnp.int32, (chunk, chunk), 1)
    causal = r >= c
    for g in range(groups):
        w = jnp.where(causal, ws_ref[g], 0.0).astype(BF16)
        cols = slice(g * gd, (g + 1) * gd)
        for n in range(tm // chunk):
            rows = slice(n * chunk, (n + 1) * chunk)
            mixed = jnp.dot(w, v[rows, cols], preferred_element_type=F32) + bias_ref[:, cols]
            o_ref[rows, cols] = (u[rows, cols] * mixed).astype(o_ref.dtype)


def _sgu(hcat, ln_g, ln_b, w_s, b_s, tm=256):
    t = hcat.shape[0]
    groups, chunk, _ = w_s.shape
    da = ln_g.shape[0]
    gd = da // groups
    tm = _tile(t, tm)
    assert tm % chunk == 0
    bias = jnp.repeat(b_s.T, gd, axis=1)
    return pl.pallas_call(
        functools.partial(_sgu_kernel, groups=groups, chunk=chunk),
        grid=(t // tm,),
        in_specs=[pl.BlockSpec((tm, da), lambda i: (i, 0)),
                  pl.BlockSpec((tm, da), lambda i: (i, 1)),
                  pl.BlockSpec((1, da), lambda i: (0, 0)),
                  pl.BlockSpec((1, da), lambda i: (0, 0)),
                  pl.BlockSpec((groups, chunk, chunk), lambda i: (0, 0, 0)),
                  pl.BlockSpec((chunk, da), lambda i: (0, 0))],
        out_specs=pl.BlockSpec((tm, da), lambda i: (i, 0)),
        out_shape=jax.ShapeDtypeStruct((t, da), BF16),
        compiler_params=_params(1),
        name="sgu",
    )(hcat, hcat, ln_g.reshape(1, da), ln_b.reshape(1, da), w_s, bias)


def _hgrn_kernel(q_ref, f_ref, i_ref, g_ref, lb_ref, on_ref, o_ref, state_ref, *, heads, chunk):
    dh = HGRN_HEAD_DIM

    @pl.when(pl.program_id(2) == 0)
    def _():
        state_ref[...] = jnp.zeros_like(state_ref)

    row = lax.broadcasted_iota(jnp.int32, (chunk, dh), 0)
    tt = lax.broadcasted_iota(jnp.int32, (chunk, chunk), 0)
    ss = lax.broadcasted_iota(jnp.int32, (chunk, chunk), 1)
    level = jnp.where(tt > ss, 31 - lax.clz(tt ^ ss), jnp.where(tt == ss, -1, -2))
    nt = (((1,), (1,)), ((), ()))
    tn = (((0,), (0,)), ((), ()))

    for h in range(heads):
        cols = slice(h * dh, (h + 1) * dh)
        z = f_ref[:, cols]
        lb = lb_ref[:, cols]
        sig = jax.nn.sigmoid(z)
        f = lb + (1.0 - lb) * sig
        logf = jnp.log(jnp.maximum(f, F_FLOOR))
        k = (1.0 - lb) * jax.nn.sigmoid(-z)
        q = q_ref[:, cols]
        q = q * jax.nn.sigmoid(q)
        v = i_ref[:, cols].astype(BF16)

        b = logf
        sh = 1
        while sh < chunk:
            b = b + jnp.where(row >= sh, pltpu.roll(b, sh, axis=0), 0.0)
            sh *= 2

        scores = jnp.where(level == -1, lax.dot_general(q.astype(BF16), k.astype(BF16), nt,
                                                        preferred_element_type=F32), 0.0)
        before = jnp.where(row >= 1, pltpu.roll(b, 1, axis=0), 0.0)
        last = b
        m, lg = 1, 0
        while m < chunk:
            qd = (q * jnp.exp(b - before)).astype(BF16)
            kd = (k * jnp.exp(last - b)).astype(BF16)
            lvl = lax.dot_general(qd, kd, nt, preferred_element_type=F32)
            scores = jnp.where(level == lg, lvl, scores)
            bit = (row & m) != 0
            before = jnp.where(bit, pltpu.roll(before, m, axis=0), before)
            last = jnp.where(bit, last, pltpu.roll(last, chunk - m, axis=0))
            m, lg = 2 * m, lg + 1
        state_t = state_ref[h]
        o = jnp.dot(scores.astype(BF16), v, preferred_element_type=F32)
        o = o + lax.dot_general((q * jnp.exp(b)).astype(BF16), state_t.astype(BF16), nt,
                                preferred_element_type=F32)
        kd = (k * jnp.exp(last - b)).astype(BF16)
        state_ref[h] = state_t * jnp.exp(last[0:1, :]) + lax.dot_general(v, kd, tn, preferred_element_type=F32)

        o = o * lax.rsqrt(jnp.mean(o * o, axis=-1, keepdims=True) + LN_EPS) * on_ref[:, cols]
        gate = g_ref[:, cols]
        o_ref[:, cols] = (o * (gate * jax.nn.sigmoid(gate))).astype(o_ref.dtype)


def _hgrn(hcat, lb, onorm_g, batch, seq, d_a, chunk=128, heads=4):
    t = hcat.shape[0]
    d_b = lb.shape[0]
    n_heads = d_b // HGRN_HEAD_DIM
    heads = _tile(n_heads, heads)
    bw = heads * HGRN_HEAD_DIM
    chunk = _tile(seq, chunk)
    assert (2 * d_a) % bw == 0 and d_b % bw == 0
    n_chunks = seq // chunk
    base = (2 * d_a) // bw
    per = d_b // bw

    def col(which):
        return pl.BlockSpec((chunk, bw), lambda b, hb, c: (b * n_chunks + c, base + which * per + hb))

    vec = pl.BlockSpec((1, bw), lambda b, hb, c: (0, hb))
    return pl.pallas_call(
        functools.partial(_hgrn_kernel, heads=heads, chunk=chunk),
        grid=(batch, n_heads // heads, n_chunks),
        in_specs=[col(0), col(1), col(2), col(3), vec, vec],
        out_specs=pl.BlockSpec((chunk, bw), lambda b, hb, c: (b * n_chunks + c, hb)),
        out_shape=jax.ShapeDtypeStruct((t, d_b), BF16),
        scratch_shapes=[pltpu.VMEM((heads, HGRN_HEAD_DIM, HGRN_HEAD_DIM), F32)],
        compiler_params=_params(3),
        name="hgrn2",
    )(hcat, hcat, hcat, hcat, lb.reshape(1, d_b), onorm_g.reshape(1, d_b))


def _xattn_kernel(x_ref, wq_ref, k_ref, v_ref, o_ref, *, heads):
    q = jnp.dot(x_ref[...], wq_ref[...], preferred_element_type=F32).astype(BF16)
    d = q.shape[1]
    dh = d // heads
    nt = (((1,), (1,)), ((), ()))
    for h in range(heads):
        cols = slice(h * dh, (h + 1) * dh)
        s = lax.dot_general(q[:, cols], k_ref[:, cols], nt, preferred_element_type=F32) * (dh ** -0.5)
        s = s - jnp.max(s, axis=-1, keepdims=True)
        p = jnp.exp(s)
        p = p / jnp.sum(p, axis=-1, keepdims=True)
        o_ref[:, cols] = jnp.dot(p.astype(BF16), v_ref[:, cols], preferred_element_type=F32).astype(o_ref.dtype)


def _xattn(xb, w_q, kv, batch, seq, tm=512):
    t, d = xb.shape
    m_len = kv.shape[0] // batch
    tm = _tile(seq, tm)
    nt = seq // tm
    return pl.pallas_call(
        functools.partial(_xattn_kernel, heads=XA_HEADS),
        grid=(batch, nt),
        in_specs=[pl.BlockSpec((tm, d), lambda b, i: (b * nt + i, 0)),
                  pl.BlockSpec((d, d), lambda b, i: (0, 0)),
                  pl.BlockSpec((m_len, d), lambda b, i: (b, 0)),
                  pl.BlockSpec((m_len, d), lambda b, i: (b, 1))],
        out_specs=pl.BlockSpec((tm, d), lambda b, i: (b * nt + i, 0)),
        out_shape=jax.ShapeDtypeStruct((t, d), BF16),
        compiler_params=_params(2),
        name="xattn",
    )(xb, w_q, kv, kv)


def _compare_exchange(xs, i, j):
    a, b = xs[i], xs[j]
    xs[i], xs[j] = jnp.maximum(a, b), jnp.minimum(a, b)


def _bitonic_merge_desc(xs):
    n = len(xs)
    j = n // 2
    while j >= 1:
        for i in range(n):
            if i ^ j > i:
                _compare_exchange(xs, i, i ^ j)
        j //= 2


def _bitonic_sort_desc(xs):
    n = len(xs)
    k = 2
    while k <= n:
        j = k // 2
        while j >= 1:
            for i in range(n):
                l = i ^ j
                if l > i:
                    if i & k == 0 or k == n:
                        _compare_exchange(xs, i, l)
                    else:
                        _compare_exchange(xs, l, i)
            j //= 2
        k *= 2


def _top_merge_desc(xs, ys):
    n = len(xs)
    zs = [jnp.maximum(xs[r], ys[n - 1 - r]) for r in range(n)]
    _bitonic_merge_desc(zs)
    return zs


def _top16_rows(s):
    nk = s.shape[0]
    assert nk == 8 * PEER_TOPK
    xs = [s[8 * g:8 * (g + 1), :] for g in range(PEER_TOPK)]
    _bitonic_sort_desc(xs)
    for shift in (4, 2, 1):
        xs = _top_merge_desc(xs, [pltpu.roll(x, shift, axis=0) for x in xs])
    return xs


def _count_true_prefix(pred, ts):
    t8 = pred(ts[7])
    t4 = pred(jnp.where(t8, ts[11], ts[3]))
    t2 = pred(jnp.where(t8, jnp.where(t4, ts[13], ts[9]), jnp.where(t4, ts[5], ts[1])))
    hi = jnp.where(t4, jnp.where(t2, ts[14], ts[12]), jnp.where(t2, ts[10], ts[8]))
    lo = jnp.where(t4, jnp.where(t2, ts[6], ts[4]), jnp.where(t2, ts[2], ts[0]))
    t1 = pred(jnp.where(t8, hi, lo))
    n = (jnp.where(t8, 8.0, 0.0) + jnp.where(t4, 4.0, 0.0)) + (jnp.where(t2, 2.0, 0.0) + jnp.where(t1, 1.0, 0.0))
    return jnp.where(pred(ts[15]), 16.0, n)


def _peer_route_kernel(x_ref, wq_ref, keys_ref, n1_ref, c1_ref, r2_ref, e2_ref, q_ref):
    k = PEER_TOPK
    q_ref[...] = jnp.dot(x_ref[...], wq_ref[...], preferred_element_type=F32).astype(q_ref.dtype)
    half = q_ref.shape[1] // 2
    nk = keys_ref.shape[1]
    nt = (((1,), (1,)), ((), ()))
    width = min(2 * LANES, q_ref.shape[0])

    def lane_tile(lt, carry):
        q = q_ref[pl.ds(pl.multiple_of(lt * width, width), width), :]
        s1 = lax.dot_general(keys_ref[0], q[:, :half], nt, preferred_element_type=F32)
        s2 = lax.dot_general(keys_ref[1], q[:, half:], nt, preferred_element_type=F32)
        a = _top16_rows(s1)
        b = _top16_rows(s2)
        first = [a[0] + b[j] for j in range(k)]
        rest = [a[i] + b[j] for i in range(1, k) for j in range(k) if (i + 1) * (j + 1) <= k]
        neg = jnp.full_like(a[0], -jnp.inf)
        rest = rest + [neg] * (3 * k - len(rest))
        lists = [rest[0:k], rest[k:2 * k], rest[2 * k:3 * k]]
        for l in lists:
            _bitonic_sort_desc(l)
        best = _top_merge_desc(_top_merge_desc(first, lists[0]), _top_merge_desc(lists[1], lists[2]))
        z = jnp.exp(best[0] - best[0])
        for r in range(1, k):
            z = z + jnp.exp(best[r] - best[0])

        rep = lambda slab: jnp.broadcast_to(slab[0:1, :], (nk, width))
        theta = rep(best[k - 1])
        bs = [rep(x) for x in b]
        n1 = _count_true_prefix(lambda br: s1 + br >= theta, bs)
        r2 = _count_true_prefix(lambda br: s2 < br, bs).astype(r2_ref.dtype)
        c1 = jnp.exp(s1 - rep(a[0])) * rep(1.0 / z)
        e2 = jnp.exp(s2 - bs[0]).astype(e2_ref.dtype)
        for w in range(width // LANES):
            tile = lt * (width // LANES) + w
            lanes = slice(w * LANES, (w + 1) * LANES)
            n1_ref[0, tile] = n1[:, lanes]
            r2_ref[0, tile] = r2[:, lanes]
            c1_ref[0, tile] = c1[:, lanes]
            e2_ref[0, tile] = e2[:, lanes]
        return carry

    lax.fori_loop(0, q_ref.shape[0] // width, lane_tile, 0)


def _peer_route(xb, w_q, keys, tm=1024):
    t, d = xb.shape
    nk, half = keys.shape[1], keys.shape[2]
    tm = _tile(t, tm)
    assert tm % LANES == 0
    out = jax.ShapeDtypeStruct((PEER_HEADS, t // LANES, nk, LANES), F32)
    out_b = jax.ShapeDtypeStruct((PEER_HEADS, t // LANES, nk, LANES), BF16)
    ospec = pl.BlockSpec((1, tm // LANES, nk, LANES), lambda i, h: (h, i, 0, 0))
    return pl.pallas_call(
        _peer_route_kernel,
        grid=(t // tm, PEER_HEADS),
        in_specs=[pl.BlockSpec((tm, d), lambda i, h: (i, 0)),
                  pl.BlockSpec((d, 2 * half), lambda i, h: (0, h)),
                  pl.BlockSpec((2, nk, half), lambda i, h: (0, 0, 0))],
        out_specs=[ospec, ospec, ospec, ospec],
        out_shape=[out, out, out_b, out_b],
        scratch_shapes=[pltpu.VMEM((tm, 2 * half), BF16)],
        compiler_params=_params(2),
        name="peer_route",
    )(xb, w_q, keys)


def _expert_gates(gate_ref, n1_ref, c1_ref, r2_ref, e2_ref):
    heads, n_tiles, nk, _ = r2_ref.shape
    zero0 = jnp.zeros((BF16_ROWS, LANES), BF16)
    zero = zero0
    for i in range(n1_ref.shape[2]):
        for lt in range(n_tiles):
            lanes = slice(lt * LANES, (lt + 1) * LANES)
            for jt in range(nk // BF16_ROWS):
                jrows = slice(jt * BF16_ROWS, (jt + 1) * BF16_ROWS)
                gate = None
                for h in range(heads):
                    nb = jnp.broadcast_to(n1_ref[h, lt, i:i + 1, :], (BF16_ROWS, LANES)).astype(BF16)
                    cb = jnp.broadcast_to(c1_ref[h, lt, i:i + 1, :], (BF16_ROWS, LANES)).astype(BF16)
                    term = jnp.where(r2_ref[h, lt, jrows, :] < nb, e2_ref[h, lt, jrows, :] * cb, zero)
                    gate = term if gate is None else gate + term
                zero = jnp.minimum(gate, zero0)
                gate_ref[i * nk + jt * BF16_ROWS:i * nk + (jt + 1) * BF16_ROWS, lanes] = gate


def _peer_expert_kernel(xt_ref, u_ref, vt_ref, n1_ref, c1_ref, r2_ref, e2_ref, h_ref, g_ref, b_ref,
                        of_ref, ob_ref, acc_ref, act_ref, hid_ref, *, alpha):
    k = pl.program_id(1)

    @pl.when(k == 0)
    def _():
        acc_ref[...] = jnp.zeros_like(acc_ref)

    _expert_gates(hid_ref, n1_ref, c1_ref, r2_ref, e2_ref)
    act_ref[...] = jnp.dot(u_ref[...], xt_ref[...], preferred_element_type=F32).astype(act_ref.dtype)
    step = LANES
    for r in range(0, hid_ref.shape[0], step):
        hid_ref[r:r + step, :] = jax.nn.gelu(act_ref[r:r + step, :]) * hid_ref[r:r + step, :]
    acc_ref[...] += jnp.dot(vt_ref[...], hid_ref[...], preferred_element_type=F32)

    @pl.when(k == pl.num_programs(1) - 1)
    def _():
        out = _layer_norm_rows(alpha * h_ref[...] + acc_ref[...].T, g_ref[...], b_ref[...])
        of_ref[...] = out
        ob_ref[...] = out.astype(BF16)


def _peer_experts_ln(xt, u, vt, n1, c1, r2, e2, h, g, b, alpha, tm=512):
    d, t = xt.shape
    ne = u.shape[0]
    heads, _, nk, _ = n1.shape
    tm = _tile(t, tm)
    rows = F32_ROWS
    te = rows * nk
    assert ne % te == 0 and nk % BF16_ROWS == 0 and tm % LANES == 0
    full_spec = pl.BlockSpec((heads, tm // LANES, nk, LANES), lambda i, k: (0, i, 0, 0))
    row_spec = pl.BlockSpec((heads, tm // LANES, rows, LANES), lambda i, k: (0, i, k, 0))
    tok_spec = pl.BlockSpec((tm, d), lambda i, k: (i, 0))
    vec_spec = pl.BlockSpec((1, d), lambda i, k: (0, 0))
    return pl.pallas_call(
        functools.partial(_peer_expert_kernel, alpha=alpha),
        grid=(t // tm, ne // te),
        in_specs=[pl.BlockSpec((d, tm), lambda i, k: (0, i)),
                  pl.BlockSpec((te, d), lambda i, k: (k, 0)),
                  pl.BlockSpec((d, te), lambda i, k: (0, k)),
                  row_spec, row_spec, full_spec, full_spec,
                  pl.BlockSpec((tm, d), lambda i, k: (i, 0), pipeline_mode=pl.Buffered(1)),
                  vec_spec, vec_spec],
        out_specs=[tok_spec, tok_spec],
        out_shape=[jax.ShapeDtypeStruct((t, d), F32), jax.ShapeDtypeStruct((t, d), BF16)],
        scratch_shapes=[pltpu.VMEM((d, tm), F32), pltpu.VMEM((te, tm), BF16), pltpu.VMEM((te, tm), BF16)],
        compiler_params=_params(2),
        name="peer_experts",
    )(xt, u, vt, n1, c1, r2, e2, h, g.reshape(1, d), b.reshape(1, d))


def kernel(x, mem, ev_w_in, ev_sgu_ln_g, ev_sgu_ln_b, ev_w_s, ev_b_s, ev_lb_logits, ev_onorm_g, ev_w_out, od_w_in, od_conv_w, od_w_out, mix_ln_g, mix_ln_b, xa_w_q, xa_w_kv, xa_w_o, xa_ln_g, xa_ln_b, peer_w_q, peer_keys, peer_u, peer_v, ffn_ln_g, ffn_ln_b):
    batch, seq, d = x.shape
    depth = mix_ln_g.shape[0]
    alpha = (2.0 * depth) ** 0.25
    t = batch * seq
    d_a = ev_sgu_ln_g.shape[1]

    p = jax.nn.softmax(ev_lb_logits.astype(F32), axis=0)
    lower_bounds = jnp.clip(jnp.cumsum(p, axis=0) - p[0], 0.0, 1.0)

    h = x.reshape(t, d)
    hb = h.astype(BF16)
    memb = mem.reshape(-1, d).astype(BF16)
    for layer in range(depth):
        j = layer // 2
        if layer % 2 == 0:
            hcat = _matmul(hb, ev_w_in[j].astype(BF16), F32)
            a_out = _sgu(hcat, ev_sgu_ln_g[j], ev_sgu_ln_b[j], ev_w_s[j], ev_b_s[j])
            b_out = _hgrn(hcat, lower_bounds[j], ev_onorm_g[j], batch, seq, d_a)
            w_out = ev_w_out[j].astype(BF16)
            us, ws = [a_out, b_out], [w_out[:d_a], w_out[d_a:]]
        else:
            us = [_shortconv(hb, od_w_in[j].astype(BF16), od_conv_w[j], seq)]
            ws = [od_w_out[j].astype(BF16)]
        h, hb = _proj_res_ln(us, ws, h, mix_ln_g[layer], mix_ln_b[layer], alpha)

        kv = _matmul(memb, xa_w_kv[layer].astype(BF16), BF16)
        att = _xattn(hb, xa_w_q[layer].astype(BF16), kv, batch, seq)
        h, hb = _proj_res_ln([att], [xa_w_o[layer].astype(BF16)], h, xa_ln_g[layer], xa_ln_b[layer], alpha)

        n1, c1, r2, e2 = _peer_route(hb, peer_w_q[layer].astype(BF16), peer_keys[layer].astype(BF16))
        h, hb = _peer_experts_ln(hb.T, peer_u[layer].astype(BF16), peer_v[layer].T.astype(BF16), n1, c1, r2, e2,
                                 h, ffn_ln_g[layer], ffn_ln_b[layer], alpha)
    return h.reshape(batch, seq, d)
```

```python
import functools

import jax
import jax.numpy as jnp
from jax import lax
from jax.experimental import pallas as pl
from jax.experimental.pallas import tpu as pltpu

LN_EPS = 1e-5
F_FLOOR = 1e-30
HGRN_HEAD_DIM = 128
XA_HEADS = 4
PEER_HEADS = 8
PEER_TOPK = 16
V7X_VMEM_LIMIT_BYTES = 56 * 1024 * 1024
LANES = 128
F32_ROWS = 8
BF16_ROWS = 16
F32 = jnp.float32
BF16 = jnp.bfloat16


def _params(n_axes, flags=None):
    return pltpu.CompilerParams(dimension_semantics=("arbitrary",) * n_axes,
                                vmem_limit_bytes=V7X_VMEM_LIMIT_BYTES, flags=flags)


def _tile(n, want):
    t = min(n, want)
    while n % t:
        t -= LANES
    assert t > 0, (n, want)
    return t


def _layer_norm_rows(y, g, b):
    mu = jnp.mean(y, axis=-1, keepdims=True)
    yc = y - mu
    var = jnp.mean(yc * yc, axis=-1, keepdims=True)
    return yc * lax.rsqrt(var + LN_EPS) * g + b


def _mm_kernel(x_ref, w_ref, o_ref):
    o_ref[...] = jnp.dot(x_ref[...], w_ref[...], preferred_element_type=F32).astype(o_ref.dtype)


def _matmul(x, w, out_dtype, tm=1024, tn=1024):
    m, k = x.shape
    n = w.shape[1]
    tm, tn = _tile(m, tm), _tile(n, tn)
    return pl.pallas_call(
        _mm_kernel,
        grid=(n // tn, m // tm),
        in_specs=[pl.BlockSpec((tm, k), lambda j, i: (i, 0)),
                  pl.BlockSpec((k, tn), lambda j, i: (0, j))],
        out_specs=pl.BlockSpec((tm, tn), lambda j, i: (i, j)),
        out_shape=jax.ShapeDtypeStruct((m, n), out_dtype),
        compiler_params=_params(2),
        name="matmul",
    )(x, w)


def _proj_ln_kernel(*refs, n_in, alpha):
    us, ws = refs[:n_in], refs[n_in:2 * n_in]
    h_ref, g_ref, b_ref, of_ref, ob_ref = refs[2 * n_in:]
    acc = jnp.dot(us[0][...], ws[0][...], preferred_element_type=F32)
    for u_ref, w_ref in zip(us[1:], ws[1:]):
        acc = acc + jnp.dot(u_ref[...], w_ref[...], preferred_element_type=F32)
    out = _layer_norm_rows(alpha * h_ref[...] + acc, g_ref[...], b_ref[...])
    of_ref[...] = out
    ob_ref[...] = out.astype(BF16)


def _proj_res_ln(us, ws, h, g, b, alpha, tm=256):
    t, d = h.shape
    tm = _tile(t, tm)
    n_in = len(us)
    in_specs = [pl.BlockSpec((tm, u.shape[1]), lambda i: (i, 0)) for u in us]
    in_specs += [pl.BlockSpec(w.shape, lambda i: (0, 0)) for w in ws]
    in_specs += [pl.BlockSpec((tm, d), lambda i: (i, 0)),
                 pl.BlockSpec((1, d), lambda i: (0, 0)),
                 pl.BlockSpec((1, d), lambda i: (0, 0))]
    return pl.pallas_call(
        functools.partial(_proj_ln_kernel, n_in=n_in, alpha=alpha),
        grid=(t // tm,),
        in_specs=in_specs,
        out_specs=[pl.BlockSpec((tm, d), lambda i: (i, 0)), pl.BlockSpec((tm, d), lambda i: (i, 0))],
        out_shape=[jax.ShapeDtypeStruct((t, d), F32), jax.ShapeDtypeStruct((t, d), BF16)],
        compiler_params=_params(1),
        name="proj_res_ln",
    )(*us, *ws, h, g.reshape(1, d), b.reshape(1, d))


def _shortconv_kernel(x_ref, wb_ref, wc_ref, wx_ref, cw_ref, o_ref, carry_ref, *, tiles_per_seq):
    i = pl.program_id(1)

    @pl.when(i % tiles_per_seq == 0)
    def _():
        carry_ref[...] = jnp.zeros_like(carry_ref)

    x = x_ref[...]
    gate_b = jnp.dot(x, wb_ref[...], preferred_element_type=F32)
    gate_c = jnp.dot(x, wc_ref[...], preferred_element_type=F32)
    xin = jnp.dot(x, wx_ref[...], preferred_element_type=F32)
    z = gate_c * xin
    tm = z.shape[0]
    prev = carry_ref[...]
    p1, p2 = prev[7:8, :], prev[6:7, :]
    row = lax.broadcasted_iota(jnp.int32, z.shape, 0)
    z1 = jnp.where(row == 0, p1, pltpu.roll(z, 1, axis=0))
    z2 = jnp.where(row == 0, p2, jnp.where(row == 1, p1, pltpu.roll(z, 2, axis=0)))
    cw = cw_ref[...]
    y = cw[0:1, :] * z2 + cw[1:2, :] * z1 + cw[2:3, :] * z
    o_ref[...] = (gate_b * y).astype(o_ref.dtype)
    carry_ref[...] = z[tm - 8:tm, :]


def _shortconv(xb, w_in, conv_w, seq, tm=512, tc=512):
    t, d = xb.shape
    tm, tc = _tile(seq, tm), _tile(d, tc)
    nc = d // tc
    cw = jnp.zeros((8, d), F32).at[:conv_w.shape[0]].set(conv_w)
    wspec = lambda off: pl.BlockSpec((d, tc), lambda c, i: (0, off + c))
    return pl.pallas_call(
        functools.partial(_shortconv_kernel, tiles_per_seq=seq // tm),
        grid=(nc, t // tm),
        in_specs=[pl.BlockSpec((tm, d), lambda c, i: (i, 0)),
                  wspec(0), wspec(nc), wspec(2 * nc),
                  pl.BlockSpec((8, tc), lambda c, i: (0, c))],
        out_specs=pl.BlockSpec((tm, tc), lambda c, i: (i, c)),
        out_shape=jax.ShapeDtypeStruct((t, d), BF16),
        scratch_shapes=[pltpu.VMEM((8, tc), F32)],
        compiler_params=_params(2),
        name="shortconv",
    )(xb, w_in, w_in, w_in, cw)


def _sgu_kernel(au_ref, av_ref, g_ref, b_ref, ws_ref, bias_ref, o_ref, *, groups, chunk):
    u = jax.nn.gelu(au_ref[...])
    v = _layer_norm_rows(jax.nn.gelu(av_ref[...]), g_ref[...], b_ref[...]).astype(BF16)
    tm, da = u.shape
    gd = da // groups
    r = lax.broadcasted_iota(jnp.int32, (chunk, chunk), 0)
    c = lax.broadcasted_iota(jnp.int32, (chunk, chunk), 1)
    causal = r >= c
    for g in range(groups):
        w = jnp.where(causal, ws_ref[g], 0.0).astype(BF16)
        cols = slice(g * gd, (g + 1) * gd)
        for n in range(tm // chunk):
            rows = slice(n * chunk, (n + 1) * chunk)
            mixed = jnp.dot(w, v[rows, cols], preferred_element_type=F32) + bias_ref[:, cols]
            o_ref[rows, cols] = (u[rows, cols] * mixed).astype(o_ref.dtype)


def _sgu(hcat, ln_g, ln_b, w_s, b_s, tm=256):
    t = hcat.shape[0]
    groups, chunk, _ = w_s.shape
    da = ln_g.shape[0]
    gd = da // groups
    tm = _tile(t, tm)
    assert tm % chunk == 0
    bias = jnp.repeat(b_s.T, gd, axis=1)
    return pl.pallas_call(
        functools.partial(_sgu_kernel, groups=groups, chunk=chunk),
        grid=(t // tm,),
        in_specs=[pl.BlockSpec((tm, da), lambda i: (i, 0)),
                  pl.BlockSpec((tm, da), lambda i: (i, 1)),
                  pl.BlockSpec((1, da), lambda i: (0, 0)),
                  pl.BlockSpec((1, da), lambda i: (0, 0)),
                  pl.BlockSpec((groups, chunk, chunk), lambda i: (0, 0, 0)),
                  pl.BlockSpec((chunk, da), lambda i: (0, 0))],
        out_specs=pl.BlockSpec((tm, da), lambda i: (i, 0)),
        out_shape=jax.ShapeDtypeStruct((t, da), BF16),
        compiler_params=_params(1),
        name="sgu",
    )(hcat, hcat, ln_g.reshape(1, da), ln_b.reshape(1, da), w_s, bias)


def _hgrn_kernel(q_ref, f_ref, i_ref, g_ref, lb_ref, on_ref, o_ref, state_ref, *, heads, chunk):
    dh = HGRN_HEAD_DIM

    @pl.when(pl.program_id(2) == 0)
    def _():
        state_ref[...] = jnp.zeros_like(state_ref)

    row = lax.broadcasted_iota(jnp.int32, (chunk, dh), 0)
    tt = lax.broadcasted_iota(jnp.int32, (chunk, chunk), 0)
    ss = lax.broadcasted_iota(jnp.int32, (chunk, chunk), 1)
    level = jnp.where(tt > ss, 31 - lax.clz(tt ^ ss), jnp.where(tt == ss, -1, -2))
    nt = (((1,), (1,)), ((), ()))
    tn = (((0,), (0,)), ((), ()))

    for h in range(heads):
        cols = slice(h * dh, (h + 1) * dh)
        z = f_ref[:, cols]
        lb = lb_ref[:, cols]
        sig = jax.nn.sigmoid(z)
        f = lb + (1.0 - lb) * sig
        logf = jnp.log(jnp.maximum(f, F_FLOOR))
        k = (1.0 - lb) * jax.nn.sigmoid(-z)
        q = q_ref[:, cols]
        q = q * jax.nn.sigmoid(q)
        v = i_ref[:, cols].astype(BF16)

        b = logf
        sh = 1
        while sh < chunk:
            b = b + jnp.where(row >= sh, pltpu.roll(b, sh, axis=0), 0.0)
            sh *= 2

        scores = jnp.where(level == -1, lax.dot_general(q.astype(BF16), k.astype(BF16), nt,
                                                        preferred_element_type=F32), 0.0)
        before = jnp.where(row >= 1, pltpu.roll(b, 1, axis=0), 0.0)
        last = b
        m, lg = 1, 0
        while m < chunk:
            qd = (q * jnp.exp(b - before)).astype(BF16)
            kd = (k * jnp.exp(last - b)).astype(BF16)
            lvl = lax.dot_general(qd, kd, nt, preferred_element_type=F32)
            scores = jnp.where(level == lg, lvl, scores)
            bit = (row & m) != 0
            before = jnp.where(bit, pltpu.roll(before, m, axis=0), before)
            last = jnp.where(bit, last, pltpu.roll(last, chunk - m, axis=0))
            m, lg = 2 * m, lg + 1
        state_t = state_ref[h]
        o = jnp.dot(scores.astype(BF16), v, preferred_element_type=F32)
        o = o + lax.dot_general((q * jnp.exp(b)).astype(BF16), state_t.astype(BF16), nt,
                                preferred_element_type=F32)
        kd = (k * jnp.exp(last - b)).astype(BF16)
        state_ref[h] = state_t * jnp.exp(last[0:1, :]) + lax.dot_general(v, kd, tn, preferred_element_type=F32)

        o = o * lax.rsqrt(jnp.mean(o * o, axis=-1, keepdims=True) + LN_EPS) * on_ref[:, cols]
        gate = g_ref[:, cols]
        o_ref[:, cols] = (o * (gate * jax.nn.sigmoid(gate))).astype(o_ref.dtype)


def _hgrn(hcat, lb, onorm_g, batch, seq, d_a, chunk=128, heads=4):
    t = hcat.shape[0]
    d_b = lb.shape[0]
    n_heads = d_b // HGRN_HEAD_DIM
    heads = _tile(n_heads, heads)
    bw = heads * HGRN_HEAD_DIM
    chunk = _tile(seq, chunk)
    assert (2 * d_a) % bw == 0 and d_b % bw == 0
    n_chunks = seq // chunk
    base = (2 * d_a) // bw
    per = d_b // bw

    def col(which):
        return pl.BlockSpec((chunk, bw), lambda b, hb, c: (b * n_chunks + c, base + which * per + hb))

    vec = pl.BlockSpec((1, bw), lambda b, hb, c: (0, hb))
    return pl.pallas_call(
        functools.partial(_hgrn_kernel, heads=heads, chunk=chunk),
        grid=(batch, n_heads // heads, n_chunks),
        in_specs=[col(0), col(1), col(2), col(3), vec, vec],
        out_specs=pl.BlockSpec((chunk, bw), lambda b, hb, c: (b * n_chunks + c, hb)),
        out_shape=jax.ShapeDtypeStruct((t, d_b), BF16),
        scratch_shapes=[pltpu.VMEM((heads, HGRN_HEAD_DIM, HGRN_HEAD_DIM), F32)],
        compiler_params=_params(3),
        name="hgrn2",
    )(hcat, hcat, hcat, hcat, lb.reshape(1, d_b), onorm_g.reshape(1, d_b))


def _xattn_kernel(x_ref, wq_ref, k_ref, v_ref, o_ref, *, heads):
    q = jnp.dot(x_ref[...], wq_ref[...], preferred_element_type=F32).astype(BF16)
    d = q.shape[1]
    dh = d // heads
    nt = (((1,), (1,)), ((), ()))
    for h in range(heads):
        cols = slice(h * dh, (h + 1) * dh)
        s = lax.dot_general(q[:, cols], k_ref[:, cols], nt, preferred_element_type=F32) * (dh ** -0.5)
        s = s - jnp.max(s, axis=-1, keepdims=True)
        p = jnp.exp(s)
        p = p / jnp.sum(p, axis=-1, keepdims=True)
        o_ref[:, cols] = jnp.dot(p.astype(BF16), v_ref[:, cols], preferred_element_type=F32).astype(o_ref.dtype)


def _xattn(xb, w_q, kv, batch, seq, tm=512):
    t, d = xb.shape
    m_len = kv.shape[0] // batch
    tm = _tile(seq, tm)
    nt = seq // tm
    return pl.pallas_call(
        functools.partial(_xattn_kernel, heads=XA_HEADS),
        grid=(batch, nt),
        in_specs=[pl.BlockSpec((tm, d), lambda b, i: (b * nt + i, 0)),
                  pl.BlockSpec((d, d), lambda b, i: (0, 0)),
                  pl.BlockSpec((m_len, d), lambda b, i: (b, 0)),
                  pl.BlockSpec((m_len, d), lambda b, i: (b, 1))],
        out_specs=pl.BlockSpec((tm, d), lambda b, i: (b * nt + i, 0)),
        out_shape=jax.ShapeDtypeStruct((t, d), BF16),
        compiler_params=_params(2),
        name="xattn",
    )(xb, w_q, kv, kv)


def _compare_exchange(xs, i, j):
    a, b = xs[i], xs[j]
    xs[i], xs[j] = jnp.maximum(a, b), jnp.minimum(a, b)


def _bitonic_merge_desc(xs):
    n = len(xs)
    j = n // 2
    while j >= 1:
        for i in range(n):
            if i ^ j > i:
                _compare_exchange(xs, i, i ^ j)
        j //= 2


def _bitonic_sort_desc(xs):
    n = len(xs)
    k = 2
    while k <= n:
        j = k // 2
        while j >= 1:
            for i in range(n):
                l = i ^ j
                if l > i:
                    if i & k == 0 or k == n:
                        _compare_exchange(xs, i, l)
                    else:
                        _compare_exchange(xs, l, i)
            j //= 2
        k *= 2


def _top_merge_desc(xs, ys):
    n = len(xs)
    zs = [jnp.maximum(xs[r], ys[n - 1 - r]) for r in range(n)]
    _bitonic_merge_desc(zs)
    return zs


def _top16_rows(s):
    nk = s.shape[0]
    assert nk == 8 * PEER_TOPK
    xs = [s[8 * g:8 * (g + 1), :] for g in range(PEER_TOPK)]
    _bitonic_sort_desc(xs)
    for shift in (4, 2, 1):
        xs = _top_merge_desc(xs, [pltpu.roll(x, shift, axis=0) for x in xs])
    return xs


def _count_true_prefix(pred, ts):
    t8 = pred(ts[7])
    t4 = pred(jnp.where(t8, ts[11], ts[3]))
    t2 = pred(jnp.where(t8, jnp.where(t4, ts[13], ts[9]), jnp.where(t4, ts[5], ts[1])))
    hi = jnp.where(t4, jnp.where(t2, ts[14], ts[12]), jnp.where(t2, ts[10], ts[8]))
    lo = jnp.where(t4, jnp.where(t2, ts[6], ts[4]), jnp.where(t2, ts[2], ts[0]))
    t1 = pred(jnp.where(t8, hi, lo))
    n = (jnp.where(t8, 8.0, 0.0) + jnp.where(t4, 4.0, 0.0)) + (jnp.where(t2, 2.0, 0.0) + jnp.where(t1, 1.0, 0.0))
    return jnp.where(pred(ts[15]), 16.0, n)


def _peer_route_kernel(x_ref, wq_ref, keys_ref, n1_ref, c1_ref, r2_ref, e2_ref, q_ref):
    k = PEER_TOPK
    q_ref[...] = jnp.dot(x_ref[...], wq_ref[...], preferred_element_type=F32).astype(q_ref.dtype)
    half = q_ref.shape[1] // 2
    nk = keys_ref.shape[1]
    nt = (((1,), (1,)), ((), ()))
    width = min(2 * LANES, q_ref.shape[0])

    def lane_tile(lt, carry):
        q = q_ref[pl.ds(pl.multiple_of(lt * width, width), width), :]
        s1 = lax.dot_general(keys_ref[0], q[:, :half], nt, preferred_element_type=F32)
        s2 = lax.dot_general(keys_ref[1], q[:, half:], nt, preferred_element_type=F32)
        a = _top16_rows(s1)
        b = _top16_rows(s2)
        first = [a[0] + b[j] for j in range(k)]
        rest = [a[i] + b[j] for i in range(1, k) for j in range(k) if (i + 1) * (j + 1) <= k]
        neg = jnp.full_like(a[0], -jnp.inf)
        rest = rest + [neg] * (3 * k - len(rest))
        lists = [rest[0:k], rest[k:2 * k], rest[2 * k:3 * k]]
        for l in lists:
            _bitonic_sort_desc(l)
        best = _top_merge_desc(_top_merge_desc(first, lists[0]), _top_merge_desc(lists[1], lists[2]))
        z = jnp.exp(best[0] - best[0])
        for r in range(1, k):
            z = z + jnp.exp(best[r] - best[0])

        rep = lambda slab: jnp.broadcast_to(slab[0:1, :], (nk, width))
        theta = rep(best[k - 1])
        bs = [rep(x) for x in b]
        n1 = _count_true_prefix(lambda br: s1 + br >= theta, bs)
        r2 = _count_true_prefix(lambda br: s2 < br, bs).astype(r2_ref.dtype)
        c1 = jnp.exp(s1 - rep(a[0])) * rep(1.0 / z)
        e2 = jnp.exp(s2 - bs[0]).astype(e2_ref.dtype)
        for w in range(width // LANES):
            tile = lt * (width // LANES) + w
            lanes = slice(w * LANES, (w + 1) * LANES)
            n1_ref[0, tile] = n1[:, lanes]
            r2_ref[0, tile] = r2[:, lanes]
            c1_ref[0, tile] = c1[:, lanes]
            e2_ref[0, tile] = e2[:, lanes]
        return carry

    lax.fori_loop(0, q_ref.shape[0] // width, lane_tile, 0)


def _peer_route(xb, w_q, keys, tm=1024):
    t, d = xb.shape
    nk, half = keys.shape[1], keys.shape[2]
    tm = _tile(t, tm)
    assert tm % LANES == 0
    out = jax.ShapeDtypeStruct((PEER_HEADS, t // LANES, nk, LANES), F32)
    out_b = jax.ShapeDtypeStruct((PEER_HEADS, t // LANES, nk, LANES), BF16)
    ospec = pl.BlockSpec((1, tm // LANES, nk, LANES), lambda i, h: (h, i, 0, 0))
    return pl.pallas_call(
        _peer_route_kernel,
        grid=(t // tm, PEER_HEADS),
        in_specs=[pl.BlockSpec((tm, d), lambda i, h: (i, 0)),
                  pl.BlockSpec((d, 2 * half), lambda i, h: (0, h)),
                  pl.BlockSpec((2, nk, half), lambda i, h: (0, 0, 0))],
        out_specs=[ospec, ospec, ospec, ospec],
        out_shape=[out, out, out_b, out_b],
        scratch_shapes=[pltpu.VMEM((tm, 2 * half), BF16)],
        compiler_params=_params(2),
        name="peer_route",
    )(xb, w_q, keys)


def _expert_gates(gate_ref, n1_ref, c1_ref, r2_ref, e2_ref):
    heads, n_tiles, nk, _ = r2_ref.shape
    zero0 = jnp.zeros((BF16_ROWS, LANES), BF16)
    zero = zero0
    for lt in range(n_tiles):
        lanes = slice(lt * LANES, (lt + 1) * LANES)
        for i in range(n1_ref.shape[2]):
            for jt in range(nk // BF16_ROWS):
                jrows = slice(jt * BF16_ROWS, (jt + 1) * BF16_ROWS)
                gate = None
                for h in range(heads):
                    nb = jnp.broadcast_to(n1_ref[h, lt, i:i + 1, :], (BF16_ROWS, LANES)).astype(BF16)
                    cb = jnp.broadcast_to(c1_ref[h, lt, i:i + 1, :], (BF16_ROWS, LANES)).astype(BF16)
                    term = jnp.where(r2_ref[h, lt, jrows, :] < nb, e2_ref[h, lt, jrows, :] * cb, zero)
                    gate = term if gate is None else gate + term
                zero = jnp.minimum(gate, zero0)
                gate_ref[i * nk + jt * BF16_ROWS:i * nk + (jt + 1) * BF16_ROWS, lanes] = gate


def _peer_expert_kernel(xt_ref, u_ref, vt_ref, n1_ref, c1_ref, r2_ref, e2_ref, h_ref, g_ref, b_ref,
                        of_ref, ob_ref, acc_ref, act_ref, hid_ref, *, alpha):
    k = pl.program_id(1)

    @pl.when(k == 0)
    def _():
        acc_ref[...] = jnp.zeros_like(acc_ref)

    _expert_gates(hid_ref, n1_ref, c1_ref, r2_ref, e2_ref)
    act_ref[...] = jnp.dot(u_ref[...], xt_ref[...], preferred_element_type=F32).astype(act_ref.dtype)
    step = LANES
    for r in range(0, hid_ref.shape[0], step):
        hid_ref[r:r + step, :] = jax.nn.gelu(act_ref[r:r + step, :]) * hid_ref[r:r + step, :]
    acc_ref[...] += jnp.dot(vt_ref[...], hid_ref[...], preferred_element_type=F32)

    @pl.when(k == pl.num_programs(1) - 1)
    def _():
        out = _layer_norm_rows(alpha * h_ref[...] + acc_ref[...].T, g_ref[...], b_ref[...])
        of_ref[...] = out
        ob_ref[...] = out.astype(BF16)


def _peer_experts_ln(xb, u, v, n1, c1, r2, e2, h, g, b, alpha, tm=512):
    t, d = xb.shape
    ne = u.shape[0]
    heads, _, nk, _ = n1.shape
    tm = _tile(t, tm)
    rows = F32_ROWS
    te = rows * nk
    assert ne % te == 0 and nk % BF16_ROWS == 0 and tm % LANES == 0
    xt = xb.reshape(t // tm, tm, d).transpose(0, 2, 1)
    vt = v.reshape(ne // te, te, d).transpose(0, 2, 1)
    full_spec = pl.BlockSpec((heads, tm // LANES, nk, LANES), lambda i, k: (0, i, 0, 0))
    row_spec = pl.BlockSpec((heads, tm // LANES, rows, LANES), lambda i, k: (0, i, k, 0))
    tok_spec = pl.BlockSpec((tm, d), lambda i, k: (i, 0))
    vec_spec = pl.BlockSpec((1, d), lambda i, k: (0, 0))
    return pl.pallas_call(
        functools.partial(_peer_expert_kernel, alpha=alpha),
        grid=(t // tm, ne // te),
        in_specs=[pl.BlockSpec((None, d, tm), lambda i, k: (i, 0, 0)),
                  pl.BlockSpec((te, d), lambda i, k: (k, 0)),
                  pl.BlockSpec((None, d, te), lambda i, k: (k, 0, 0)),
                  row_spec, row_spec, full_spec, full_spec,
                  pl.BlockSpec((tm, d), lambda i, k: (i, 0), pipeline_mode=pl.Buffered(1)),
                  vec_spec, vec_spec],
        out_specs=[tok_spec, tok_spec],
        out_shape=[jax.ShapeDtypeStruct((t, d), F32), jax.ShapeDtypeStruct((t, d), BF16)],
        scratch_shapes=[pltpu.VMEM((d, tm), F32), pltpu.VMEM((te, tm), BF16), pltpu.VMEM((te, tm), BF16)],
        compiler_params=_params(2),
        name="peer_experts",
    )(xt, u, vt, n1, c1, r2, e2, h, g.reshape(1, d), b.reshape(1, d))


def kernel(x, mem, ev_w_in, ev_sgu_ln_g, ev_sgu_ln_b, ev_w_s, ev_b_s, ev_lb_logits, ev_onorm_g, ev_w_out, od_w_in, od_conv_w, od_w_out, mix_ln_g, mix_ln_b, xa_w_q, xa_w_kv, xa_w_o, xa_ln_g, xa_ln_b, peer_w_q, peer_keys, peer_u, peer_v, ffn_ln_g, ffn_ln_b):
    batch, seq, d = x.shape
    depth = mix_ln_g.shape[0]
    alpha = (2.0 * depth) ** 0.25
    t = batch * seq
    d_a = ev_sgu_ln_g.shape[1]

    p = jax.nn.softmax(ev_lb_logits.astype(F32), axis=0)
    lower_bounds = jnp.clip(jnp.cumsum(p, axis=0) - p[0], 0.0, 1.0)

    h = x.reshape(t, d)
    hb = h.astype(BF16)
    memb = mem.reshape(-1, d).astype(BF16)
    for layer in range(depth):
        j = layer // 2
        if layer % 2 == 0:
            hcat = _matmul(hb, ev_w_in[j].astype(BF16), F32)
            a_out = _sgu(hcat, ev_sgu_ln_g[j], ev_sgu_ln_b[j], ev_w_s[j], ev_b_s[j])
            b_out = _hgrn(hcat, lower_bounds[j], ev_onorm_g[j], batch, seq, d_a)
            w_out = ev_w_out[j].astype(BF16)
            us, ws = [a_out, b_out], [w_out[:d_a], w_out[d_a:]]
        else:
            us = [_shortconv(hb, od_w_in[j].astype(BF16), od_conv_w[j], seq)]
            ws = [od_w_out[j].astype(BF16)]
        h, hb = _proj_res_ln(us, ws, h, mix_ln_g[layer], mix_ln_b[layer], alpha)

        kv = _matmul(memb, xa_w_kv[layer].astype(BF16), BF16)
        att = _xattn(hb, xa_w_q[layer].astype(BF16), kv, batch, seq)
        h, hb = _proj_res_ln([att], [xa_w_o[layer].astype(BF16)], h, xa_ln_g[layer], xa_ln_b[layer], alpha)

        n1, c1, r2, e2 = _peer_route(hb, peer_w_q[layer].astype(BF16), peer_keys[layer].astype(BF16))
        h, hb = _peer_experts_ln(hb, peer_u[layer].astype(BF16), peer_v[layer].astype(BF16), n1, c1, r2, e2,
                                 h, ffn_ln_g[layer], ffn_ln_b[layer], alpha)
    return h.reshape(batch, seq, d)
```
